```python
import jax, jax.numpy as jnp
from jax import lax
import numpy as np

D_MODEL = 2048
BATCH = 2
SEQ = 4096
DEPTH = 4
DEC_BATCH = 8
DEC_SEQ = 4
PAST_LEN = 16384
PAGE_SIZE = 128

N_A_LAYERS = DEPTH // 2
N_B_LAYERS = DEPTH - N_A_LAYERS
CONV_WIDTH = 31
CONV_STATE = CONV_WIDTH - 1
GROUPS = ((128, 1), (512, 4), (2048, 16))
N_GROUPS = len(GROUPS)
N_HEADS = 16
HEAD_DIM = D_MODEL // N_HEADS
ROT_DIM = HEAD_DIM // 4
ROPE_THETA = 500000.0
QBLK = max(w // d for w, d in GROUPS)
D_FF = ((8 * D_MODEL + 3 * 256 - 1) // (3 * 256)) * 256
RMS_EPS = 1e-6
LN_EPS = 1e-5
NEG = -1e30

kernel_name = "yoco_conformer_dilated_window_decoder_step"


def rmsnorm(x, g):
    xf = x.astype(jnp.float32)
    y = xf * lax.rsqrt(jnp.mean(xf * xf, axis=-1, keepdims=True) + RMS_EPS)
    return (y * g.astype(jnp.float32)).astype(x.dtype)


def layernorm(x, g, b):
    xf = x.astype(jnp.float32)
    mu = jnp.mean(xf, axis=-1, keepdims=True)
    var = jnp.mean(jnp.square(xf - mu), axis=-1, keepdims=True)
    y = (xf - mu) * lax.rsqrt(var + LN_EPS)
    return (y * g.astype(jnp.float32) + b.astype(jnp.float32)).astype(x.dtype)


def rotary(x, pos):
    half = ROT_DIM // 2
    inv = ROPE_THETA ** (-jnp.arange(half, dtype=jnp.float32) / half)
    ang = pos.astype(jnp.float32)[:, None] * inv[None, :]
    shape = (pos.shape[0],) + (1,) * (x.ndim - 3) + (half,)
    cos = jnp.cos(ang).reshape(shape)
    sin = jnp.sin(ang).reshape(shape)
    xr = x[..., :ROT_DIM].astype(jnp.float32)
    x1, x2 = xr[..., :half], xr[..., half:]
    rot = jnp.concatenate([x1 * cos - x2 * sin, x2 * cos + x1 * sin], axis=-1).astype(x.dtype)
    return jnp.concatenate([rot, x[..., ROT_DIM:]], axis=-1)


def conv_module(h, state, w_in, b_in, w_dw, b_dw, ln_g, ln_b, w_out, b_out):
    u = h @ w_in + b_in
    a, gate = jnp.split(u, 2, axis=-1)
    g = a * jax.nn.sigmoid(gate)
    buf = jnp.concatenate([state.astype(g.dtype), g], axis=1)
    y = lax.conv_general_dilated(buf, w_dw[:, None, :].astype(buf.dtype), window_strides=(1,),
                                 padding="VALID", dimension_numbers=("NWC", "WIO", "NWC"),
                                 feature_group_count=buf.shape[-1]) + b_dw
    y = jax.nn.silu(layernorm(y, ln_g, ln_b))
    return y @ w_out + b_out, buf[:, -CONV_STATE:]


def swiglu(h, w_in, w_out):
    gt, up = jnp.split(h @ w_in, 2, axis=-1)
    return (jax.nn.silu(gt) * up) @ w_out


def attend(s, mask, v, spec):
    s = jnp.where(mask, s, NEG)
    m = jnp.max(s, axis=-1, keepdims=True)
    p = jnp.exp(s - m)
    den = jnp.sum(p, axis=-1, keepdims=True)
    o = jnp.einsum(spec, (p / den).astype(v.dtype), v)
    lse = (m + jnp.log(den))[..., 0]
    return o, lse


def dilated_window_prompt(q, k, v, window, dilation):
    B, S, H, Dh = q.shape
    n_back = window // dilation
    L = S // dilation
    N = B * dilation

    def to_res(t):
        t = t.reshape(B, L, dilation, *t.shape[2:])
        return jnp.moveaxis(t, 2, 1).reshape(N, L, *t.shape[3:])

    nb = -(-L // QBLK)
    Lp = nb * QBLK
    pad = ((0, 0), (0, Lp - L), (0, 0), (0, 0))
    qb, kb, vb = (jnp.pad(to_res(t), pad).reshape(N, nb, QBLK, H, Dh) for t in (q, k, v))
    zeros = jnp.zeros_like(kb[:, :1])
    kk = jnp.concatenate([jnp.concatenate([zeros, kb[:, :-1]], axis=1), kb], axis=2)
    vv = jnp.concatenate([jnp.concatenate([zeros, vb[:, :-1]], axis=1), vb], axis=2)
    s = jnp.einsum("nbqhd,nbkhd->nbhqk", qb, kk, preferred_element_type=jnp.float32) * (HEAD_DIM ** -0.5)
    blk = jnp.arange(nb)[:, None] * QBLK
    qpos = blk + jnp.arange(QBLK)[None, :]
    kpos = blk - QBLK + jnp.arange(2 * QBLK)[None, :]
    rel = qpos[:, :, None] - kpos[:, None, :]
    mask = (rel >= 0) & (rel <= n_back) & (kpos[:, None, :] >= 0)
    o, lse = attend(s, mask[None, :, None], vv, "nbhqk,nbkhd->nbqhd")
    o = o.reshape(N, Lp, H, Dh)[:, :L]
    lse = jnp.transpose(lse, (0, 1, 3, 2)).reshape(N, Lp, H)[:, :L]

    def from_res(t):
        t = t.reshape(B, dilation, L, *t.shape[2:])
        return jnp.moveaxis(t, 1, 2).reshape(B, S, *t.shape[3:])

    return from_res(o), from_res(lse)


def dilated_window_sample(q, kv_buf, window, dilation):
    T = q.shape[1]
    Wb = kv_buf.shape[1] - T
    n_back = window // dilation
    idx = Wb + jnp.arange(T)[:, None] - dilation * jnp.arange(n_back + 1)[None, :]
    valid = idx >= 0
    gath = kv_buf[:, jnp.clip(idx, 0)]
    s = jnp.einsum("bthd,btkhd->bhtk", q, gath[:, :, :, 0], preferred_element_type=jnp.float32) * (HEAD_DIM ** -0.5)
    o, lse = attend(s, valid[None, None], gath[:, :, :, 1], "bhtk,btkhd->bthd")
    return o, jnp.transpose(lse, (0, 2, 1))


def merge_groups(outs, lses, w_o):
    o = jnp.stack(outs, axis=2)
    lse = jnp.stack(lses, axis=2)
    wgt = jax.nn.softmax(lse.astype(jnp.float32), axis=2).astype(o.dtype)
    o = jnp.sum(o * wgt[..., None], axis=2)
    return o.reshape(o.shape[0], o.shape[1], N_HEADS * HEAD_DIM) @ w_o


def run_trunk(x, pos, conv_state, kv_caches, norm_mix, norm_ffn, conv_w_in, conv_b_in, conv_w_dw,
              conv_b_dw, conv_ln_g, conv_ln_b, conv_w_out, conv_b_out, norm_kv, w_kv, attn_w_q,
              attn_w_o, ffn_w_in, ffn_w_out, norm_final):
    B, T, _ = x.shape
    new_conv = []
    ctx = None
    new_kv = None
    for l in range(DEPTH):
        if l < N_A_LAYERS:
            y, st = conv_module(rmsnorm(x, norm_mix[l]), conv_state[l], conv_w_in[l], conv_b_in[l],
                                conv_w_dw[l], conv_b_dw[l], conv_ln_g[l], conv_ln_b[l],
                                conv_w_out[l], conv_b_out[l])
            x = x + y
            new_conv.append(st)
        else:
            if l == N_A_LAYERS:
                kv = (rmsnorm(x, norm_kv) @ w_kv).reshape(B, T, 2, N_GROUPS, N_HEADS, HEAD_DIM)
                kv = jnp.stack([rotary(kv[:, :, 0], pos), kv[:, :, 1]], axis=2)
                if kv_caches is None:
                    ctx = [kv[:, :, :, g] for g in range(N_GROUPS)]
                    new_kv = [kv[:, T - min(w, T):, :, g] for g, (w, _) in enumerate(GROUPS)]
                else:
                    ctx = [jnp.concatenate([kv_caches[g].astype(kv.dtype), kv[:, :, :, g]], axis=1)
                           for g in range(N_GROUPS)]
                    new_kv = [c[:, T:] for c in ctx]
            j = l - N_A_LAYERS
            q = (rmsnorm(x, norm_mix[l]) @ attn_w_q[j]).reshape(B, T, N_GROUPS, N_HEADS, HEAD_DIM)
            q = rotary(q, pos)
            outs, lses = [], []
            for g, (w, d) in enumerate(GROUPS):
                if kv_caches is None:
                    o, lse = dilated_window_prompt(q[:, :, g], ctx[g][:, :, 0], ctx[g][:, :, 1], w, d)
                else:
                    o, lse = dilated_window_sample(q[:, :, g], ctx[g], w, d)
                outs.append(o)
                lses.append(lse)
            x = x + merge_groups(outs, lses, attn_w_o[j])
        x = x + swiglu(rmsnorm(x, norm_ffn[l]), ffn_w_in[l], ffn_w_out[l])
    return rmsnorm(x, norm_final), jnp.stack(new_conv, axis=0), new_kv


def setup_inputs(seed: int = 0) -> dict:
    key = jax.random.key(seed)
    ks = jax.random.split(key, 24)

    def nrm(k, shape, scale):
        return jax.random.normal(k, shape, jnp.float32) * scale

    qw = N_GROUPS * N_HEADS * HEAD_DIM
    wb = [min(w, PAST_LEN) for w, _ in GROUPS]
    return {
        "x_prompt": nrm(ks[0], (BATCH, SEQ, D_MODEL), 1.0),
        "x_sample": nrm(ks[1], (DEC_BATCH, DEC_SEQ, D_MODEL), 1.0),
        "state_conv": nrm(ks[2], (N_A_LAYERS, DEC_BATCH, CONV_STATE, D_MODEL), 0.5),
        "cache_kv_g0": nrm(ks[3], (DEC_BATCH, wb[0], 2, N_HEADS, HEAD_DIM), 1.0),
        "cache_kv_g1": nrm(ks[4], (DEC_BATCH, wb[1], 2, N_HEADS, HEAD_DIM), 1.0),
        "cache_kv_g2": nrm(ks[5], (DEC_BATCH, wb[2], 2, N_HEADS, HEAD_DIM), 1.0),
        "norm_mix": 1.0 + nrm(ks[6], (DEPTH, D_MODEL), 0.05),
        "norm_ffn": 1.0 + nrm(ks[7], (DEPTH, D_MODEL), 0.05),
        "conv_w_in": nrm(ks[8], (N_A_LAYERS, D_MODEL, 2 * D_MODEL), D_MODEL ** -0.5),
        "conv_b_in": nrm(ks[9], (N_A_LAYERS, 2 * D_MODEL), 0.02),
        "conv_w_dw": nrm(ks[10], (N_A_LAYERS, CONV_WIDTH, D_MODEL), CONV_WIDTH ** -0.5),
        "conv_b_dw": nrm(ks[11], (N_A_LAYERS, D_MODEL), 0.02),
        "conv_ln_g": 1.0 + nrm(ks[12], (N_A_LAYERS, D_MODEL), 0.05),
        "conv_ln_b": nrm(ks[13], (N_A_LAYERS, D_MODEL), 0.02),
        "conv_w_out": nrm(ks[14], (N_A_LAYERS, D_MODEL, D_MODEL), D_MODEL ** -0.5),
        "conv_b_out": nrm(ks[15], (N_A_LAYERS, D_MODEL), 0.02),
        "norm_kv": 1.0 + nrm(ks[16], (D_MODEL,), 0.05),
        "w_kv": nrm(ks[17], (D_MODEL, 2 * qw), D_MODEL ** -0.5),
        "attn_w_q": nrm(ks[18], (N_B_LAYERS, D_MODEL, qw), D_MODEL ** -0.5),
        "attn_w_o": nrm(ks[19], (N_B_LAYERS, N_HEADS * HEAD_DIM, D_MODEL), (N_HEADS * HEAD_DIM) ** -0.5),
        "ffn_w_in": nrm(ks[20], (DEPTH, D_MODEL, 2 * D_FF), D_MODEL ** -0.5),
        "ffn_w_out": nrm(ks[21], (DEPTH, D_FF, D_MODEL), D_FF ** -0.5),
        "norm_final": 1.0 + nrm(ks[22], (D_MODEL,), 0.05),
    }


def reference(x_prompt, x_sample, state_conv, cache_kv_g0, cache_kv_g1, cache_kv_g2, norm_mix,
              norm_ffn, conv_w_in, conv_b_in, conv_w_dw, conv_b_dw, conv_ln_g, conv_ln_b,
              conv_w_out, conv_b_out, norm_kv, w_kv, attn_w_q, attn_w_o, ffn_w_in, ffn_w_out,
              norm_final):
    weights = (norm_mix, norm_ffn, conv_w_in, conv_b_in, conv_w_dw, conv_b_dw, conv_ln_g, conv_ln_b,
               conv_w_out, conv_b_out, norm_kv, w_kv, attn_w_q, attn_w_o, ffn_w_in, ffn_w_out,
               norm_final)
    bp, sp = x_prompt.shape[0], x_prompt.shape[1]
    pos_p = jnp.arange(sp, dtype=jnp.int32)
    conv0 = jnp.zeros((N_A_LAYERS, bp, CONV_STATE, D_MODEL), x_prompt.dtype)
    y_prompt, conv_p, kv_p = run_trunk(x_prompt, pos_p, conv0, None, *weights)
    kv0_p, kv1_p, kv2_p = kv_p
    pos_s = PAST_LEN + jnp.arange(x_sample.shape[1], dtype=jnp.int32)
    y_sample, conv_s, kv_s = run_trunk(x_sample, pos_s, state_conv,
                                       (cache_kv_g0, cache_kv_g1, cache_kv_g2), *weights)
    kv0_s, kv1_s, kv2_s = kv_s
    return (y_prompt, y_sample, conv_p, kv0_p, kv1_p, kv2_p, conv_s, kv0_s, kv1_s, kv2_s)
```

```python
import functools

import jax
import jax.numpy as jnp
from jax import lax
from jax.experimental import pallas as pl
from jax.experimental.pallas import tpu as pltpu

F32 = jnp.float32
BF16 = jnp.bfloat16

D_MODEL = 2048
N_HEADS = 16
HEAD_DIM = 128
GROUPS = ((128, 1), (512, 4), (2048, 16))
N_GROUPS = len(GROUPS)
PAST_LEN = 16384
QBLK = 128
ROT_DIM = HEAD_DIM // 4
ROPE_THETA = 500000.0
CONV_WIDTH = 31
CONV_STATE = CONV_WIDTH - 1
CONV_HALO = 32
CONV_TAPS = 40
RMS_EPS = 1e-6
LN_EPS = 1e-5
NEG = -1e30
SM_SCALE = HEAD_DIM ** -0.5
QW = N_GROUPS * N_HEADS * HEAD_DIM

V7X_VMEM_LIMIT = 56 * 1024 * 1024


def _params(n_axes):
    return pltpu.CompilerParams(dimension_semantics=("arbitrary",) * n_axes,
                                vmem_limit_bytes=V7X_VMEM_LIMIT)


def _rms(x, g):
    return x * lax.rsqrt(jnp.mean(x * x, axis=-1, keepdims=True) + RMS_EPS) * g


def _glu_kernel(x_ref, gn_ref, wa_ref, wg_ref, ba_ref, bg_ref, o_ref, xn_ref):
    @pl.when(pl.program_id(1) == 0)
    def _():
        xn_ref[...] = _rms(x_ref[...], gn_ref[...]).astype(BF16)

    xn = xn_ref[...]
    a = jnp.dot(xn, wa_ref[...], preferred_element_type=F32) + ba_ref[...]
    gate = jnp.dot(xn, wg_ref[...], preferred_element_type=F32) + bg_ref[...]
    o_ref[...] = a * jax.nn.sigmoid(gate)


def _glu(x, gn, w_in, b_in, tm, tn=512):
    m, d = x.shape
    nj = d // tn
    return pl.pallas_call(
        _glu_kernel,
        out_shape=jax.ShapeDtypeStruct((m, d), F32),
        grid=(m // tm, nj),
        in_specs=[
            pl.BlockSpec((tm, d), lambda i, j: (i, 0)),
            pl.BlockSpec((1, d), lambda i, j: (0, 0)),
            pl.BlockSpec((d, tn), lambda i, j: (0, j)),
            pl.BlockSpec((d, tn), lambda i, j: (0, j + nj)),
            pl.BlockSpec((1, tn), lambda i, j: (0, j)),
            pl.BlockSpec((1, tn), lambda i, j: (0, j + nj)),
        ],
        out_specs=pl.BlockSpec((tm, tn), lambda i, j: (i, j)),
        scratch_shapes=[pltpu.VMEM((tm, d), BF16)],
        compiler_params=_params(2),
        name="glu_proj",
    )(x, gn, w_in, w_in, b_in, b_in)


def _conv_taps(buf_ref, w_ref, r0, c0, rows):
    acc = None
    for s in range(8):
        win = rows if s == 0 else rows + 8
        u = None
        for a in range(CONV_TAPS // 8):
            k = 8 * a + s
            if k < CONV_HALO - CONV_STATE or k > CONV_HALO:
                continue
            term = buf_ref[pl.ds(r0 + 8 * a, win), pl.ds(c0, 128)] * w_ref[k:k + 1, pl.ds(c0, 128)]
            u = term if u is None else u + term
        part = u if s == 0 else u[s:s + rows]
        acc = part if acc is None else acc + part
    return acc


def _ln_silu(y, g, b):
    mu = jnp.mean(y, axis=-1, keepdims=True)
    dlt = y - mu
    var = jnp.mean(dlt * dlt, axis=-1, keepdims=True)
    z = dlt * lax.rsqrt(var + LN_EPS) * g + b
    return z * jax.nn.sigmoid(z)


def _conv_prompt_kernel(cur_ref, halo_ref, w_ref, bdw_ref, lg_ref, lb_ref, o_ref,
                        buf_ref, y_ref, *, tiles_per_seq, tm, rc):
    i = pl.program_id(0)
    buf_ref[0:CONV_HALO, :] = jnp.where(i % tiles_per_seq == 0, 0.0, halo_ref[...])
    buf_ref[CONV_HALO:, :] = cur_ref[...]

    def col_body(c, carry):
        c0 = pl.multiple_of(c * 128, 128)

        def row_body(r, carry2):
            r0 = pl.multiple_of(r * rc, rc)
            y_ref[pl.ds(r0, rc), pl.ds(c0, 128)] = _conv_taps(buf_ref, w_ref, r0, c0, rc)
            return carry2

        return lax.fori_loop(0, tm // rc, row_body, carry)

    lax.fori_loop(0, D_MODEL // 128, col_body, 0)
    y = y_ref[...] + bdw_ref[...]
    o_ref[...] = _ln_silu(y, lg_ref[...], lb_ref[...]).astype(o_ref.dtype)


def _conv_prompt(g, w_taps, b_dw, ln_g, ln_b, seq, tm=256, rc=64):
    m, d = g.shape
    hb = tm // CONV_HALO
    kern = functools.partial(_conv_prompt_kernel, tiles_per_seq=seq // tm, tm=tm, rc=rc)
    return pl.pallas_call(
        kern,
        out_shape=jax.ShapeDtypeStruct((m, d), BF16),
        grid=(m // tm,),
        in_specs=[
            pl.BlockSpec((tm, d), lambda i: (i, 0)),
            pl.BlockSpec((CONV_HALO, d), lambda i: (jnp.maximum(i * hb - 1, 0), 0)),
            pl.BlockSpec((CONV_TAPS, d), lambda i: (0, 0)),
            pl.BlockSpec((1, d), lambda i: (0, 0)),
            pl.BlockSpec((1, d), lambda i: (0, 0)),
            pl.BlockSpec((1, d), lambda i: (0, 0)),
        ],
        out_specs=pl.BlockSpec((tm, d), lambda i: (i, 0)),
        scratch_shapes=[pltpu.VMEM((tm + CONV_HALO, d), F32), pltpu.VMEM((tm, d), F32)],
        compiler_params=_params(1),
        name="conv_prompt",
    )(g, g, w_taps, b_dw, ln_g, ln_b)


def _conv_sample_kernel(buf_ref, w_ref, bdw_ref, lg_ref, lb_ref, o_ref, *, t_new):
    acc = None
    for k in range(CONV_HALO - CONV_STATE, CONV_HALO + 1):
        term = buf_ref[k:k + t_new, :] * w_ref[k:k + 1, :]
        acc = term if acc is None else acc + term
    y = acc + bdw_ref[...]
    o_ref[...] = _ln_silu(y, lg_ref[...], lb_ref[...])


def _conv_sample(buf, w_taps, b_dw, ln_g, ln_b, t_new):
    nb, rows, d = buf.shape
    kern = functools.partial(_conv_sample_kernel, t_new=t_new)
    return pl.pallas_call(
        kern,
        out_shape=jax.ShapeDtypeStruct((nb, t_new, d), F32),
        grid=(nb,),
        in_specs=[
            pl.BlockSpec((None, rows, d), lambda b: (b, 0, 0)),
            pl.BlockSpec((CONV_TAPS, d), lambda b: (0, 0)),
            pl.BlockSpec((1, d), lambda b: (0, 0)),
            pl.BlockSpec((1, d), lambda b: (0, 0)),
            pl.BlockSpec((1, d), lambda b: (0, 0)),
        ],
        out_specs=pl.BlockSpec((None, t_new, d), lambda b: (b, 0, 0)),
        compiler_params=_params(1),
        name="conv_sample",
    )(buf, w_taps, b_dw, ln_g, ln_b)


def _out_proj_kernel(y_ref, w_ref, b_ref, x_ref, o_ref):
    acc = jnp.dot(y_ref[...].astype(BF16), w_ref[...], preferred_element_type=F32)
    o_ref[...] = x_ref[...] + acc + b_ref[...]


def _out_proj(y, w, b, x, tm, tn=1024):
    m, d = x.shape
    k = y.shape[1]
    return pl.pallas_call(
        _out_proj_kernel,
        out_shape=jax.ShapeDtypeStruct((m, d), F32),
        grid=(m // tm, d // tn),
        in_specs=[
            pl.BlockSpec((tm, k), lambda i, j: (i, 0)),
            pl.BlockSpec((k, tn), lambda i, j: (0, j)),
            pl.BlockSpec((1, tn), lambda i, j: (0, j)),
            pl.BlockSpec((tm, tn), lambda i, j: (i, j)),
        ],
        out_specs=pl.BlockSpec((tm, tn), lambda i, j: (i, j)),
        compiler_params=_params(2),
        name="out_proj",
    )(y, w, b, x)


def _ffn_kernel(x_ref, gn_ref, wg_ref, wu_ref, wo_ref, gf_ref, o_ref, xn_ref, *, final_norm):
    f = pl.program_id(1)

    @pl.when(f == 0)
    def _():
        xn_ref[...] = _rms(x_ref[...], gn_ref[...]).astype(BF16)

    xn = xn_ref[...]
    gt = jnp.dot(xn, wg_ref[...], preferred_element_type=F32)
    up = jnp.dot(xn, wu_ref[...], preferred_element_type=F32)
    h = (gt * jax.nn.sigmoid(gt) * up).astype(BF16)
    part = jnp.dot(h, wo_ref[...], preferred_element_type=F32)

    @pl.when(f == 0)
    def _():
        o_ref[...] = x_ref[...] + part

    @pl.when(f > 0)
    def _():
        o_ref[...] += part

    if final_norm:
        @pl.when(f == pl.num_programs(1) - 1)
        def _():
            o_ref[...] = _rms(o_ref[...], gf_ref[...])


def _ffn(x, gn, w_in, w_out, gf, tm, final_norm, tf=512):
    m, d = x.shape
    dff = w_out.shape[0]
    nf = dff // tf
    kern = functools.partial(_ffn_kernel, final_norm=final_norm)
    return pl.pallas_call(
        kern,
        out_shape=jax.ShapeDtypeStruct((m, d), F32),
        grid=(m // tm, nf),
        in_specs=[
            pl.BlockSpec((tm, d), lambda i, f: (i, 0)),
            pl.BlockSpec((1, d), lambda i, f: (0, 0)),
            pl.BlockSpec((d, tf), lambda i, f: (0, f)),
            pl.BlockSpec((d, tf), lambda i, f: (0, f + nf)),
            pl.BlockSpec((tf, d), lambda i, f: (f, 0)),
            pl.BlockSpec((1, d), lambda i, f: (0, 0)),
        ],
        out_specs=pl.BlockSpec((tm, d), lambda i, f: (i, 0)),
        scratch_shapes=[pltpu.VMEM((tm, d), BF16)],
        compiler_params=_params(2),
        name="ffn",
    )(x, gn, w_in, w_in, w_out, gf)


def _rope(acc, cos_t, sin_a, sin_b):
    outs = []
    for hh in range(acc.shape[1] // HEAD_DIM):
        xs = acc[:, hh * HEAD_DIM:(hh + 1) * HEAD_DIM]
        outs.append(xs * cos_t
                    + pltpu.roll(xs, HEAD_DIM - ROT_DIM // 2, axis=1) * sin_a
                    + pltpu.roll(xs, ROT_DIM // 2, axis=1) * sin_b)
    return jnp.concatenate(outs, axis=1)


def _proj_rope_kernel(x_ref, gn_ref, w_ref, cos_ref, sa_ref, sb_ref, *refs, n_rope, n_out):
    out_refs, xn_ref = refs[:n_out], refs[n_out]
    j = pl.program_id(1)

    @pl.when(j == 0)
    def _():
        xn_ref[...] = _rms(x_ref[...], gn_ref[...]).astype(BF16)

    acc = jnp.dot(xn_ref[...], w_ref[...], preferred_element_type=F32)

    @pl.when(j < n_rope)
    def _():
        rot = _rope(acc, cos_ref[...], sa_ref[...], sb_ref[...])
        for o_ref in out_refs:
            o_ref[...] = rot.astype(o_ref.dtype)

    @pl.when(j >= n_rope)
    def _():
        for o_ref in out_refs:
            o_ref[...] = acc.astype(o_ref.dtype)


def _proj_rope(x, gn, w, tabs, rope_cols, out_dtypes, tm, tn=512):
    m, d = x.shape
    n = w.shape[1]
    cos_t, sin_a, sin_b = tabs
    tab_blocks = cos_t.shape[0] // tm
    kern = functools.partial(_proj_rope_kernel, n_rope=rope_cols // tn, n_out=len(out_dtypes))
    tab_spec = pl.BlockSpec((tm, HEAD_DIM), lambda i, j: (i % tab_blocks, 0))
    return pl.pallas_call(
        kern,
        out_shape=[jax.ShapeDtypeStruct((m, n), dt) for dt in out_dtypes],
        grid=(m // tm, n // tn),
        in_specs=[
            pl.BlockSpec((tm, d), lambda i, j: (i, 0)),
            pl.BlockSpec((1, d), lambda i, j: (0, 0)),
            pl.BlockSpec((d, tn), lambda i, j: (0, j)),
            tab_spec, tab_spec, tab_spec,
        ],
        out_specs=[pl.BlockSpec((tm, tn), lambda i, j: (i, j)) for _ in out_dtypes],
        scratch_shapes=[pltpu.VMEM((tm, d), BF16)],
        compiler_params=_params(2),
        name="proj_rope",
    )(x, gn, w, cos_t, sin_a, sin_b)


def _rope_tables(pos):
    half = ROT_DIM // 2
    inv = ROPE_THETA ** (-jnp.arange(half, dtype=F32) / half)
    ang = pos.astype(F32)[:, None] * inv[None, :]
    cos, sin = jnp.cos(ang), jnp.sin(ang)
    n = pos.shape[0]
    ones = jnp.ones((n, HEAD_DIM - ROT_DIM), F32)
    zeros = jnp.zeros((n, HEAD_DIM - ROT_DIM), F32)
    zh = jnp.zeros((n, half), F32)
    cos_t = jnp.concatenate([cos, cos, ones], axis=1)
    sin_a = jnp.concatenate([-sin, zh, zeros], axis=1)
    sin_b = jnp.concatenate([zh, sin, zeros], axis=1)
    return cos_t, sin_a, sin_b


def _attn_prompt_kernel(q_ref, kp_ref, kc_ref, vp_ref, vc_ref, o_ref, lse_ref):
    nb = pl.program_id(2)
    qi = lax.broadcasted_iota(jnp.int32, (QBLK, QBLK), 0)
    ki = lax.broadcasted_iota(jnp.int32, (QBLK, QBLK), 1)
    mask_c = ki <= qi
    mask_p = jnp.logical_and(ki >= qi, nb > 0)
    lane = lax.broadcasted_iota(jnp.int32, (QBLK, HEAD_DIM), 1)
    nt = (((1,), (1,)), ((), ()))
    lse_tile = jnp.zeros((QBLK, HEAD_DIM), F32)
    for h in range(N_HEADS):
        cs = slice(h * HEAD_DIM, (h + 1) * HEAD_DIM)
        qh = q_ref[:, cs]
        sp = lax.dot_general(qh, kp_ref[:, cs], nt, preferred_element_type=F32) * SM_SCALE
        sc = lax.dot_general(qh, kc_ref[:, cs], nt, preferred_element_type=F32) * SM_SCALE
        sp = jnp.where(mask_p, sp, NEG)
        sc = jnp.where(mask_c, sc, NEG)
        mx = jnp.maximum(jnp.max(sp, axis=-1, keepdims=True), jnp.max(sc, axis=-1, keepdims=True))
        pp = jnp.exp(sp - mx)
        pc = jnp.exp(sc - mx)
        den = jnp.sum(pp, axis=-1, keepdims=True) + jnp.sum(pc, axis=-1, keepdims=True)
        o = (jnp.dot(pp.astype(BF16), vp_ref[:, cs], preferred_element_type=F32)
             + jnp.dot(pc.astype(BF16), vc_ref[:, cs], preferred_element_type=F32))
        o_ref[:, cs] = o / den
        lse_tile = jnp.where(lane == h, mx + jnp.log(den), lse_tile)
    lse_ref[...] = lse_tile


def _attn_prompt(q, kv, g, batch, seq):
    m = q.shape[0]
    _, dil = GROUPS[g]
    nbk = seq // dil // QBLK
    gw = N_HEADS * HEAD_DIM
    q2 = q.reshape(m // dil, dil * QW)
    kv2 = kv.reshape(m // dil, dil * 2 * QW)
    nq, nkv = QW // gw, 2 * QW // gw

    def cur(b, r, nb):
        return b * nbk + nb

    def prev(b, r, nb):
        return b * nbk + jnp.maximum(nb - 1, 0)

    o, lse = pl.pallas_call(
        _attn_prompt_kernel,
        out_shape=[jax.ShapeDtypeStruct((m // dil, dil * gw), F32),
                   jax.ShapeDtypeStruct((m // dil, dil * HEAD_DIM), F32)],
        grid=(batch, dil, nbk),
        in_specs=[
            pl.BlockSpec((QBLK, gw), lambda b, r, nb: (cur(b, r, nb), r * nq + g)),
            pl.BlockSpec((QBLK, gw), lambda b, r, nb: (prev(b, r, nb), r * nkv + g)),
            pl.BlockSpec((QBLK, gw), lambda b, r, nb: (cur(b, r, nb), r * nkv + g)),
            pl.BlockSpec((QBLK, gw), lambda b, r, nb: (prev(b, r, nb), r * nkv + N_GROUPS + g)),
            pl.BlockSpec((QBLK, gw), lambda b, r, nb: (cur(b, r, nb), r * nkv + N_GROUPS + g)),
        ],
        out_specs=[
            pl.BlockSpec((QBLK, gw), lambda b, r, nb: (cur(b, r, nb), r)),
            pl.BlockSpec((QBLK, HEAD_DIM), lambda b, r, nb: (cur(b, r, nb), r)),
        ],
        compiler_params=_params(3),
        name=f"attn_prompt_g{g}",
    )(q2, kv2, kv2, kv2, kv2)
    return o.reshape(m, gw), lse.reshape(m, HEAD_DIM)


def _merge_proj_kernel(o0_ref, o1_ref, o2_ref, l0_ref, l1_ref, l2_ref, w_ref, x_ref, out_ref, mg_ref):
    @pl.when(pl.program_id(1) == 0)
    def _():
        l0, l1, l2 = l0_ref[...], l1_ref[...], l2_ref[...]
        mx = jnp.maximum(jnp.maximum(l0, l1), l2)
        e0, e1, e2 = jnp.exp(l0 - mx), jnp.exp(l1 - mx), jnp.exp(l2 - mx)
        tot = e0 + e1 + e2
        w0, w1, w2 = e0 / tot, e1 / tot, e2 / tot
        for h in range(N_HEADS):
            cs = slice(h * HEAD_DIM, (h + 1) * HEAD_DIM)
            mh = (o0_ref[:, cs] * w0[:, h:h + 1] + o1_ref[:, cs] * w1[:, h:h + 1]
                  + o2_ref[:, cs] * w2[:, h:h + 1])
            mg_ref[:, cs] = mh.astype(BF16)

    acc = jnp.dot(mg_ref[...], w_ref[...], preferred_element_type=F32)
    out_ref[...] = x_ref[...] + acc


def _merge_proj(os, lses, w, x, tm=256, tn=1024):
    m, d = x.shape
    ospec = pl.BlockSpec((tm, d), lambda i, j: (i, 0))
    lspec = pl.BlockSpec((tm, HEAD_DIM), lambda i, j: (i, 0))
    return pl.pallas_call(
        _merge_proj_kernel,
        out_shape=jax.ShapeDtypeStruct((m, d), F32),
        grid=(m // tm, d // tn),
        in_specs=[ospec, ospec, ospec, lspec, lspec, lspec,
                  pl.BlockSpec((d, tn), lambda i, j: (0, j)),
                  pl.BlockSpec((tm, tn), lambda i, j: (i, j))],
        out_specs=pl.BlockSpec((tm, tn), lambda i, j: (i, j)),
        scratch_shapes=[pltpu.VMEM((tm, d), BF16)],
        compiler_params=_params(2),
        name="merge_proj",
    )(*os, *lses, w, x)


def _attn_sample_kernel(q_ref, new_ref, c0_ref, c1_ref, c2_ref, o_ref, *, t_new):
    t = pl.program_id(1)
    caches = (c0_ref, c1_ref, c2_ref)
    outs, lses = [], []
    for g, (win, dil) in enumerate(GROUPS):
        c_ref = caches[g]
        n_c = win // dil
        res = t & (dil - 1)
        qg = q_ref[g]
        kc, vc = c_ref[:, 0], c_ref[:, 1]
        kn, vn = new_ref[:, 0, g], new_ref[:, 1, g]
        sc = jnp.sum(kc * qg[None], axis=-1, keepdims=True) * SM_SCALE
        sn = jnp.sum(kn * qg[None], axis=-1, keepdims=True) * SM_SCALE
        mi = lax.broadcasted_iota(jnp.int32, (n_c, N_HEADS, 1), 0)
        sc = jnp.where(mi * dil + res >= t, sc, NEG)
        ti = lax.broadcasted_iota(jnp.int32, (t_new, N_HEADS, 1), 0)
        vis = jnp.logical_and(ti <= t, ((t - ti) & (dil - 1)) == 0)
        sn = jnp.where(vis, sn, NEG)
        mx = jnp.maximum(jnp.max(sc, axis=0, keepdims=True), jnp.max(sn, axis=0, keepdims=True))
        pc = jnp.exp(sc - mx)
        pn = jnp.exp(sn - mx)
        den = jnp.sum(pc, axis=0, keepdims=True) + jnp.sum(pn, axis=0, keepdims=True)
        o = (jnp.sum(pc * vc, axis=0, keepdims=True) + jnp.sum(pn * vn, axis=0, keepdims=True)) / den
        outs.append(o[0])
        lses.append((mx + jnp.log(den))[0])
    lmx = jnp.maximum(jnp.maximum(lses[0], lses[1]), lses[2])
    es = [jnp.exp(l - lmx) for l in lses]
    tot = es[0] + es[1] + es[2]
    o_ref[...] = (outs[0] * (es[0] / tot) + outs[1] * (es[1] / tot) + outs[2] * (es[2] / tot))


def _attn_sample(q, kv_new, caches, nbatch, t_new):
    q5 = q.reshape(nbatch, t_new, N_GROUPS, N_HEADS, HEAD_DIM)
    new6 = kv_new.reshape(nbatch, t_new, 2, N_GROUPS, N_HEADS, HEAD_DIM)
    cviews, cspecs = [], []
    for (win, dil), c in zip(GROUPS, caches):
        n_c = win // dil
        cviews.append(c.reshape(nbatch, n_c, dil, 2, N_HEADS, HEAD_DIM))
        cspecs.append(pl.BlockSpec((None, n_c, None, 2, N_HEADS, HEAD_DIM),
                                   lambda b, t, dil=dil: (b, 0, lax.rem(t, dil), 0, 0, 0)))
    kern = functools.partial(_attn_sample_kernel, t_new=t_new)
    o = pl.pallas_call(
        kern,
        out_shape=jax.ShapeDtypeStruct((nbatch, t_new, N_HEADS, HEAD_DIM), F32),
        grid=(nbatch, t_new),
        in_specs=[
            pl.BlockSpec((None, None, N_GROUPS, N_HEADS, HEAD_DIM), lambda b, t: (b, t, 0, 0, 0)),
            pl.BlockSpec((None, t_new, 2, N_GROUPS, N_HEADS, HEAD_DIM), lambda b, t: (b, 0, 0, 0, 0, 0)),
            *cspecs,
        ],
        out_specs=pl.BlockSpec((None, None, N_HEADS, HEAD_DIM), lambda b, t: (b, t, 0, 0)),
        compiler_params=_params(2),
        name="attn_sample",
    )(q5, new6, *cviews)
    return o.reshape(nbatch * t_new, N_HEADS * HEAD_DIM)


def _trunk(x, pos, batch, seq, conv_state, kv_caches, wts):
    m = x.shape[0]
    tm = min(512, m)
    tabs = _rope_tables(pos)
    n_a = wts["conv_w_in"].shape[0]
    depth = wts["ffn_w_in"].shape[0]
    row = lambda v: v.reshape(1, -1)
    new_conv = []
    kv_f32 = kv_b16 = None
    for l in range(depth):
        if l < n_a:
            g = _glu(x, row(wts["norm_mix"][l]), wts["conv_w_in"][l], row(wts["conv_b_in"][l]), tm)
            w_taps = jnp.pad(wts["conv_w_dw"][l],
                             ((CONV_HALO - CONV_STATE, CONV_TAPS - CONV_HALO - 1), (0, 0)))
            cargs = (w_taps, row(wts["conv_b_dw"][l]), row(wts["conv_ln_g"][l]), row(wts["conv_ln_b"][l]))
            g3 = g.reshape(batch, seq, D_MODEL)
            if conv_state is None:
                y = _conv_prompt(g, *cargs, seq=seq)
                new_conv.append(g3[:, seq - CONV_STATE:])
            else:
                st = conv_state[l]
                buf = jnp.concatenate([st, g3], axis=1)
                new_conv.append(buf[:, -CONV_STATE:])
                lead = CONV_HALO - CONV_STATE
                buf = jnp.pad(buf, ((0, 0), (lead, CONV_TAPS - lead - CONV_STATE - seq), (0, 0)))
                y = _conv_sample(buf, *cargs, t_new=seq).reshape(m, D_MODEL)
            x = _out_proj(y, wts["conv_w_out"][l], row(wts["conv_b_out"][l]), x, tm)
        else:
            j = l - n_a
            if j == 0:
                kv_f32, kv_b16 = _proj_rope(x, row(wts["norm_kv"]), wts["w_kv"], tabs, QW, (F32, BF16), tm)
            gq = row(wts["norm_mix"][l])
            if kv_caches is None:
                (q,) = _proj_rope(x, gq, wts["attn_w_q"][j], tabs, QW, (BF16,), tm)
                os, lses = zip(*[_attn_prompt(q, kv_b16, g, batch, seq) for g in range(N_GROUPS)])
                x = _merge_proj(os, lses, wts["attn_w_o"][j], x)
            else:
                (q,) = _proj_rope(x, gq, wts["attn_w_q"][j], tabs, QW, (F32,), tm)
                o = _attn_sample(q, kv_f32, kv_caches, batch, seq)
                x = _out_proj(o, wts["attn_w_o"][j], jnp.zeros((1, D_MODEL), F32), x, tm)
        x = _ffn(x, row(wts["norm_ffn"][l]), wts["ffn_w_in"][l], wts["ffn_w_out"][l],
                 row(wts["norm_final"]), tm, final_norm=(l == depth - 1))
    kv6 = kv_f32.reshape(batch, seq, 2, N_GROUPS, N_HEADS, HEAD_DIM)
    return x, jnp.stack(new_conv, axis=0), kv6


def kernel(x_prompt, x_sample, state_conv, cache_kv_g0, cache_kv_g1, cache_kv_g2, norm_mix, norm_ffn, conv_w_in, conv_b_in, conv_w_dw, conv_b_dw, conv_ln_g, conv_ln_b, conv_w_out, conv_b_out, norm_kv, w_kv, attn_w_q, attn_w_o, ffn_w_in, ffn_w_out, norm_final):
    wts = dict(norm_mix=norm_mix, norm_ffn=norm_ffn, conv_b_in=conv_b_in, conv_w_dw=conv_w_dw,
               conv_b_dw=conv_b_dw, conv_ln_g=conv_ln_g, conv_ln_b=conv_ln_b, conv_b_out=conv_b_out,
               norm_kv=norm_kv, norm_final=norm_final)
    for name, w in (("conv_w_in", conv_w_in), ("conv_w_out", conv_w_out), ("w_kv", w_kv),
                    ("attn_w_q", attn_w_q), ("attn_w_o", attn_w_o), ("ffn_w_in", ffn_w_in),
                    ("ffn_w_out", ffn_w_out)):
        wts[name] = w.astype(BF16)

    bp, sp, d = x_prompt.shape
    pos_p = jnp.arange(sp, dtype=jnp.int32)
    y_p, conv_p, kv_p = _trunk(x_prompt.reshape(bp * sp, d), pos_p, bp, sp, None, None, wts)
    kv_out_p = [kv_p[:, sp - min(w, sp):, :, g] for g, (w, _) in enumerate(GROUPS)]

    bs, ss, _ = x_sample.shape
    caches = (cache_kv_g0, cache_kv_g1, cache_kv_g2)
    pos_s = jnp.tile(PAST_LEN + jnp.arange(ss, dtype=jnp.int32), bs)
    y_s, conv_s, kv_s = _trunk(x_sample.reshape(bs * ss, d), pos_s, bs, ss, state_conv, caches, wts)
    kv_out_s = [jnp.concatenate([c[:, ss:], kv_s[:, :, :, g]], axis=1) for g, c in enumerate(caches)]

    return (y_p.reshape(bp, sp, d), y_s.reshape(bs, ss, d), conv_p, *kv_out_p, conv_s, *kv_out_s)
```

```python
import functools

import jax
import jax.numpy as jnp
from jax import lax
from jax.experimental import pallas as pl
from jax.experimental.pallas import tpu as pltpu

F32 = jnp.float32
BF16 = jnp.bfloat16

D_MODEL = 2048
N_HEADS = 16
HEAD_DIM = 128
GROUPS = ((128, 1), (512, 4), (2048, 16))
N_GROUPS = len(GROUPS)
PAST_LEN = 16384
QBLK = 128
ATT_SPAN = QBLK * max(d for _, d in GROUPS)
ROT_DIM = HEAD_DIM // 4
ROPE_THETA = 500000.0
CONV_WIDTH = 31
CONV_STATE = CONV_WIDTH - 1
CONV_HALO = 32
CONV_TAPS = 40
RMS_EPS = 1e-6
LN_EPS = 1e-5
NEG = -1e30
SM_SCALE = HEAD_DIM ** -0.5
QW = N_GROUPS * N_HEADS * HEAD_DIM

V7X_VMEM_LIMIT = 56 * 1024 * 1024


def _params(n_axes):
    return pltpu.CompilerParams(dimension_semantics=("arbitrary",) * n_axes,
                                vmem_limit_bytes=V7X_VMEM_LIMIT)


def _rms(x, g):
    return x * lax.rsqrt(jnp.mean(x * x, axis=-1, keepdims=True) + RMS_EPS) * g


def _glu_kernel(x_ref, gn_ref, wa_ref, wg_ref, ba_ref, bg_ref, o_ref, xn_ref):
    @pl.when(pl.program_id(1) == 0)
    def _():
        xn_ref[...] = _rms(x_ref[...], gn_ref[...]).astype(BF16)

    xn = xn_ref[...]
    a = jnp.dot(xn, wa_ref[...], preferred_element_type=F32) + ba_ref[...]
    gate = jnp.dot(xn, wg_ref[...], preferred_element_type=F32) + bg_ref[...]
    o_ref[...] = a * jax.nn.sigmoid(gate)


def _glu(x, gn, w_in, b_in, layer, tm, tn=512):
    m, d = x.shape
    nj = d // tn
    return pl.pallas_call(
        _glu_kernel,
        out_shape=jax.ShapeDtypeStruct((m, d), F32),
        grid=(m // tm, nj),
        in_specs=[
            pl.BlockSpec((tm, d), lambda i, j: (i, 0)),
            pl.BlockSpec((1, d), lambda i, j: (0, 0)),
            pl.BlockSpec((None, d, tn), lambda i, j: (layer, 0, j)),
            pl.BlockSpec((None, d, tn), lambda i, j: (layer, 0, j + nj)),
            pl.BlockSpec((1, tn), lambda i, j: (0, j)),
            pl.BlockSpec((1, tn), lambda i, j: (0, j + nj)),
        ],
        out_specs=pl.BlockSpec((tm, tn), lambda i, j: (i, j)),
        scratch_shapes=[pltpu.VMEM((tm, d), BF16)],
        compiler_params=_params(2),
        name="glu_proj",
    )(x, gn, w_in, w_in, b_in, b_in)


def _conv_taps(buf_ref, w_ref, r0, c0, rows):
    acc = None
    for s in range(8):
        win = rows if s == 0 else rows + 8
        u = None
        for a in range(CONV_TAPS // 8):
            k = 8 * a + s
            if k < CONV_HALO - CONV_STATE or k > CONV_HALO:
                continue
            term = buf_ref[pl.ds(r0 + 8 * a, win), pl.ds(c0, 128)] * w_ref[k:k + 1, pl.ds(c0, 128)]
            u = term if u is None else u + term
        part = u if s == 0 else u[s:s + rows]
        acc = part if acc is None else acc + part
    return acc


def _ln_silu(y, g, b):
    mu = jnp.mean(y, axis=-1, keepdims=True)
    dlt = y - mu
    var = jnp.mean(dlt * dlt, axis=-1, keepdims=True)
    z = dlt * lax.rsqrt(var + LN_EPS) * g + b
    return z * jax.nn.sigmoid(z)


def _conv_prompt_kernel(cur_ref, halo_ref, w_ref, bdw_ref, lg_ref, lb_ref, o_ref,
                        buf_ref, y_ref, *, tiles_per_seq, tm, rc):
    i = pl.program_id(0)
    buf_ref[0:CONV_HALO, :] = jnp.where(i % tiles_per_seq == 0, 0.0, halo_ref[...])
    buf_ref[CONV_HALO:, :] = cur_ref[...]

    def col_body(c, carry):
        c0 = pl.multiple_of(c * 128, 128)

        def row_body(r, carry2):
            r0 = pl.multiple_of(r * rc, rc)
            y_ref[pl.ds(r0, rc), pl.ds(c0, 128)] = _conv_taps(buf_ref, w_ref, r0, c0, rc)
            return carry2

        return lax.fori_loop(0, tm // rc, row_body, carry)

    lax.fori_loop(0, D_MODEL // 128, col_body, 0)
    y = y_ref[...] + bdw_ref[...]
    o_ref[...] = _ln_silu(y, lg_ref[...], lb_ref[...]).astype(o_ref.dtype)


def _conv_prompt(g, w_taps, b_dw, ln_g, ln_b, seq, tm=256, rc=64):
    m, d = g.shape
    hb = tm // CONV_HALO
    kern = functools.partial(_conv_prompt_kernel, tiles_per_seq=seq // tm, tm=tm, rc=rc)
    return pl.pallas_call(
        kern,
        out_shape=jax.ShapeDtypeStruct((m, d), BF16),
        grid=(m // tm,),
        in_specs=[
            pl.BlockSpec((tm, d), lambda i: (i, 0)),
            pl.BlockSpec((CONV_HALO, d), lambda i: (jnp.maximum(i * hb - 1, 0), 0)),
            pl.BlockSpec((CONV_TAPS, d), lambda i: (0, 0)),
            pl.BlockSpec((1, d), lambda i: (0, 0)),
            pl.BlockSpec((1, d), lambda i: (0, 0)),
            pl.BlockSpec((1, d), lambda i: (0, 0)),
        ],
        out_specs=pl.BlockSpec((tm, d), lambda i: (i, 0)),
        scratch_shapes=[pltpu.VMEM((tm + CONV_HALO, d), F32), pltpu.VMEM((tm, d), F32)],
        compiler_params=_params(1),
        name="conv_prompt",
    )(g, g, w_taps, b_dw, ln_g, ln_b)


def _conv_sample_kernel(buf_ref, w_ref, bdw_ref, lg_ref, lb_ref, o_ref, *, t_new):
    acc = None
    for k in range(CONV_HALO - CONV_STATE, CONV_HALO + 1):
        term = buf_ref[k:k + t_new, :] * w_ref[k:k + 1, :]
        acc = term if acc is None else acc + term
    y = acc + bdw_ref[...]
    o_ref[...] = _ln_silu(y, lg_ref[...], lb_ref[...])


def _conv_sample(buf, w_taps, b_dw, ln_g, ln_b, t_new):
    nb, rows, d = buf.shape
    kern = functools.partial(_conv_sample_kernel, t_new=t_new)
    return pl.pallas_call(
        kern,
        out_shape=jax.ShapeDtypeStruct((nb, t_new, d), F32),
        grid=(nb,),
        in_specs=[
            pl.BlockSpec((None, rows, d), lambda b: (b, 0, 0)),
            pl.BlockSpec((CONV_TAPS, d), lambda b: (0, 0)),
            pl.BlockSpec((1, d), lambda b: (0, 0)),
            pl.BlockSpec((1, d), lambda b: (0, 0)),
            pl.BlockSpec((1, d), lambda b: (0, 0)),
        ],
        out_specs=pl.BlockSpec((None, t_new, d), lambda b: (b, 0, 0)),
        compiler_params=_params(1),
        name="conv_sample",
    )(buf, w_taps, b_dw, ln_g, ln_b)


def _out_proj_kernel(y_ref, w_ref, b_ref, x_ref, o_ref):
    acc = jnp.dot(y_ref[...].astype(BF16), w_ref[...], preferred_element_type=F32)
    o_ref[...] = x_ref[...] + acc + b_ref[...]


def _out_proj(y, w, b, x, layer, tm, tn=1024):
    m, d = x.shape
    k = y.shape[1]
    return pl.pallas_call(
        _out_proj_kernel,
        out_shape=jax.ShapeDtypeStruct((m, d), F32),
        grid=(m // tm, d // tn),
        in_specs=[
            pl.BlockSpec((tm, k), lambda i, j: (i, 0)),
            pl.BlockSpec((None, k, tn), lambda i, j: (layer, 0, j)),
            pl.BlockSpec((1, tn), lambda i, j: (0, j)),
            pl.BlockSpec((tm, tn), lambda i, j: (i, j)),
        ],
        out_specs=pl.BlockSpec((tm, tn), lambda i, j: (i, j)),
        compiler_params=_params(2),
        name="out_proj",
    )(y, w, b, x)


def _ffn_kernel(x_ref, gn_ref, wg_ref, wu_ref, wo_ref, gf_ref, o_ref, xn_ref, *, final_norm):
    f = pl.program_id(1)

    @pl.when(f == 0)
    def _():
        x = x_ref[...]
        xn_ref[...] = _rms(x, gn_ref[...]).astype(BF16)
        o_ref[...] = x

    xn = xn_ref[...]
    gt = jnp.dot(xn, wg_ref[...], preferred_element_type=F32)
    up = jnp.dot(xn, wu_ref[...], preferred_element_type=F32)
    h = (gt * jax.nn.sigmoid(gt) * up).astype(BF16)
    o_ref[...] += jnp.dot(h, wo_ref[...], preferred_element_type=F32)

    if final_norm:
        @pl.when(f == pl.num_programs(1) - 1)
        def _():
            o_ref[...] = _rms(o_ref[...], gf_ref[...])


def _ffn(x, gn, w_in, w_out, gf, layer, tm, final_norm, tf=512):
    m, d = x.shape
    dff = w_out.shape[1]
    nf = dff // tf
    kern = functools.partial(_ffn_kernel, final_norm=final_norm)
    return pl.pallas_call(
        kern,
        out_shape=jax.ShapeDtypeStruct((m, d), F32),
        grid=(m // tm, nf),
        in_specs=[
            pl.BlockSpec((tm, d), lambda i, f: (i, 0)),
            pl.BlockSpec((1, d), lambda i, f: (0, 0)),
            pl.BlockSpec((None, d, tf), lambda i, f: (layer, 0, f)),
            pl.BlockSpec((None, d, tf), lambda i, f: (layer, 0, f + nf)),
            pl.BlockSpec((None, tf, d), lambda i, f: (layer, f, 0)),
            pl.BlockSpec((1, d), lambda i, f: (0, 0)),
        ],
        out_specs=pl.BlockSpec((tm, d), lambda i, f: (i, 0)),
        scratch_shapes=[pltpu.VMEM((tm, d), BF16)],
        compiler_params=_params(2),
        name="ffn",
    )(x, gn, w_in, w_in, w_out, gf)


def _rope(acc, cos_t, sin_a, sin_b):
    outs = []
    for hh in range(acc.shape[1] // HEAD_DIM):
        xs = acc[:, hh * HEAD_DIM:(hh + 1) * HEAD_DIM]
        outs.append(xs * cos_t
                    + pltpu.roll(xs, HEAD_DIM - ROT_DIM // 2, axis=1) * sin_a
                    + pltpu.roll(xs, ROT_DIM // 2, axis=1) * sin_b)
    return jnp.concatenate(outs, axis=1)


def _proj_kernel(x_ref, gn_ref, w_ref, cos_ref, sa_ref, sb_ref, o_ref, xn_ref, *, rope):
    @pl.when(pl.program_id(1) == 0)
    def _():
        xn_ref[...] = _rms(x_ref[...], gn_ref[...]).astype(BF16)

    acc = jnp.dot(xn_ref[...], w_ref[...], preferred_element_type=F32)
    if rope:
        acc = _rope(acc, cos_ref[...], sa_ref[...], sb_ref[...])
    o_ref[...] = acc


def _proj(x, gn, w, layer, col0, n, tabs, rope, tm, tn=512):
    m, d = x.shape
    cos_t, sin_a, sin_b = tabs
    tab_blocks = cos_t.shape[0] // tm
    cb0 = col0 // tn
    kern = functools.partial(_proj_kernel, rope=rope)
    tab_spec = pl.BlockSpec((tm, HEAD_DIM), lambda i, j: (i % tab_blocks, 0))
    return pl.pallas_call(
        kern,
        out_shape=jax.ShapeDtypeStruct((m, n), F32),
        grid=(m // tm, n // tn),
        in_specs=[
            pl.BlockSpec((tm, d), lambda i, j: (i, 0)),
            pl.BlockSpec((1, d), lambda i, j: (0, 0)),
            pl.BlockSpec((None, d, tn), lambda i, j: (layer, 0, cb0 + j)),
            tab_spec, tab_spec, tab_spec,
        ],
        out_specs=pl.BlockSpec((tm, tn), lambda i, j: (i, j)),
        scratch_shapes=[pltpu.VMEM((tm, d), BF16)],
        compiler_params=_params(2),
        name="proj_rope" if rope else "proj",
    )(x, gn, w, cos_t, sin_a, sin_b)


def _rope_tables(pos):
    half = ROT_DIM // 2
    inv = ROPE_THETA ** (-jnp.arange(half, dtype=F32) / half)
    ang = pos.astype(F32)[:, None] * inv[None, :]
    cos, sin = jnp.cos(ang), jnp.sin(ang)
    n = pos.shape[0]
    ones = jnp.ones((n, HEAD_DIM - ROT_DIM), F32)
    zeros = jnp.zeros((n, HEAD_DIM - ROT_DIM), F32)
    zh = jnp.zeros((n, half), F32)
    cos_t = jnp.concatenate([cos, cos, ones], axis=1)
    sin_a = jnp.concatenate([-sin, zh, zeros], axis=1)
    sin_b = jnp.concatenate([zh, sin, zeros], axis=1)
    return cos_t, sin_a, sin_b


def _attn_prompt_kernel(q0, q1, q2, kc0, kc1, kc2, vc0, vc1, vc2, kp0, kp1, kp2, vp0, vp1, vp2,
                        o_ref, s_scr, m_scr, p_scr, num_scr, den_scr, mx_scr):
    has_prev = pl.program_id(1) > 0
    qi = lax.broadcasted_iota(jnp.int32, (QBLK, QBLK), 0)
    ki = lax.broadcasted_iota(jnp.int32, (QBLK, QBLK), 1)
    mask_c = ki <= qi
    mask_p = ki >= qi
    mask_p0 = jnp.logical_and(mask_p, has_prev)
    ones = jnp.ones((QBLK, HEAD_DIM), BF16)
    nt = (((1,), (1,)), ((), ()))
    refs = ((q0, kc0, vc0, kp0, vp0), (q1, kc1, vc1, kp1, vp1), (q2, kc2, vc2, kp2, vp2))

    for g, (_, dil) in enumerate(GROUPS):
        q_ref, kc_ref, vc_ref, kp_ref, vp_ref = refs[g]
        units = [(jb, r) for jb in range(ATT_SPAN // (QBLK * dil)) for r in range(dil)]

        def rows(ref, jb, r, dil=dil):
            start = QBLK * jb * dil + r
            if dil == 1:
                return ref[pl.ds(start, QBLK), :]
            return ref[pl.ds(start, QBLK, stride=dil), :]

        def put(scr, jb, r, val, g=g, dil=dil):
            start = QBLK * jb * dil + r
            if dil == 1:
                scr[g, pl.ds(start, QBLK), :] = val
            else:
                scr[g, pl.ds(start, QBLK, stride=dil), :] = val

        for u, (jb, r) in enumerate(units):
            q = rows(q_ref, jb, r).astype(BF16)
            kc = rows(kc_ref, jb, r).astype(BF16)
            if jb > 0:
                kp, mp = rows(kc_ref, jb - 1, r).astype(BF16), mask_p
            else:
                kp, mp = rows(kp_ref, 0, r).astype(BF16), mask_p0
            sc = lax.dot_general(q, kc, nt, preferred_element_type=F32) * SM_SCALE
            sp = lax.dot_general(q, kp, nt, preferred_element_type=F32) * SM_SCALE
            sc = jnp.where(mask_c, sc, NEG)
            sp = jnp.where(mp, sp, NEG)
            s_scr[u, :, 0:QBLK] = sp
            s_scr[u, :, QBLK:] = sc
            mx = jnp.max(jnp.maximum(sp, sc), axis=-1, keepdims=True)
            m_scr[u] = jnp.broadcast_to(mx, (QBLK, HEAD_DIM))

        for u in range(len(units)):
            mx = m_scr[u]
            p_scr[u, :, 0:QBLK] = jnp.exp(s_scr[u, :, 0:QBLK] - mx).astype(BF16)
            p_scr[u, :, QBLK:] = jnp.exp(s_scr[u, :, QBLK:] - mx).astype(BF16)

        for u, (jb, r) in enumerate(units):
            vc = rows(vc_ref, jb, r).astype(BF16)
            vp = (rows(vc_ref, jb - 1, r) if jb > 0 else rows(vp_ref, 0, r)).astype(BF16)
            acc = (jnp.dot(p_scr[u, :, 0:QBLK], jnp.concatenate([vp, ones], axis=1),
                           preferred_element_type=F32)
                   + jnp.dot(p_scr[u, :, QBLK:], jnp.concatenate([vc, ones], axis=1),
                             preferred_element_type=F32))
            put(num_scr, jb, r, acc[:, 0:HEAD_DIM])
            put(den_scr, jb, r, acc[:, HEAD_DIM:])
            put(mx_scr, jb, r, m_scr[u])

    ch = 256
    for c in range(ATT_SPAN // ch):
        rs = pl.ds(c * ch, ch)
        m0, m1, m2 = mx_scr[0, rs, :], mx_scr[1, rs, :], mx_scr[2, rs, :]
        mm = jnp.maximum(jnp.maximum(m0, m1), m2)
        a0, a1, a2 = jnp.exp(m0 - mm), jnp.exp(m1 - mm), jnp.exp(m2 - mm)
        num = a0 * num_scr[0, rs, :] + a1 * num_scr[1, rs, :] + a2 * num_scr[2, rs, :]
        den = a0 * den_scr[0, rs, :] + a1 * den_scr[1, rs, :] + a2 * den_scr[2, rs, :]
        o_ref[rs, :] = (num / den).astype(o_ref.dtype)


def _attn_prompt(q, k, v, batch, seq):
    m = q.shape[0]
    assert seq % ATT_SPAN == 0
    nspan = seq // ATT_SPAN

    def cur(g):
        return pl.BlockSpec((ATT_SPAN, HEAD_DIM), lambda b, s, h: (b * nspan + s, g * N_HEADS + h))

    def prev(g):
        rows = QBLK * GROUPS[g][1]
        per_span = ATT_SPAN // rows
        return pl.BlockSpec(
            (rows, HEAD_DIM),
            lambda b, s, h: (jnp.maximum((b * nspan + s) * per_span - 1, 0), g * N_HEADS + h))

    gs = range(N_GROUPS)
    acc = lambda: pltpu.VMEM((N_GROUPS, ATT_SPAN, HEAD_DIM), F32)
    return pl.pallas_call(
        _attn_prompt_kernel,
        out_shape=jax.ShapeDtypeStruct((m, N_HEADS * HEAD_DIM), BF16),
        grid=(batch, nspan, N_HEADS),
        in_specs=([cur(g) for g in gs] * 3 + [prev(g) for g in gs] * 2),
        out_specs=pl.BlockSpec((ATT_SPAN, HEAD_DIM), lambda b, s, h: (b * nspan + s, h)),
        scratch_shapes=[
            pltpu.VMEM((N_HEADS, QBLK, 2 * QBLK), F32),
            pltpu.VMEM((N_HEADS, QBLK, HEAD_DIM), F32),
            pltpu.VMEM((N_HEADS, QBLK, 2 * QBLK), BF16),
            acc(), acc(), acc(),
        ],
        compiler_params=_params(3),
        name="attn_prompt",
    )(q, q, q, k, k, k, v, v, v, k, k, k, v, v, v)


def _attn_sample_kernel(q_ref, kn_ref, vn_ref, c0_ref, c1_ref, c2_ref, o_ref, *, t_new):
    t = pl.program_id(1)
    caches = (c0_ref, c1_ref, c2_ref)
    outs, lses = [], []
    for g, (win, dil) in enumerate(GROUPS):
        c_ref = caches[g]
        n_c = win // dil
        res = t & (dil - 1)
        qg = q_ref[g]
        kc, vc = c_ref[:, 0], c_ref[:, 1]
        kn, vn = kn_ref[:, g], vn_ref[:, g]
        sc = jnp.sum(kc * qg[None], axis=-1, keepdims=True) * SM_SCALE
        sn = jnp.sum(kn * qg[None], axis=-1, keepdims=True) * SM_SCALE
        mi = lax.broadcasted_iota(jnp.int32, (n_c, N_HEADS, 1), 0)
        sc = jnp.where(mi * dil + res >= t, sc, NEG)
        ti = lax.broadcasted_iota(jnp.int32, (t_new, N_HEADS, 1), 0)
        vis = jnp.logical_and(ti <= t, ((t - ti) & (dil - 1)) == 0)
        sn = jnp.where(vis, sn, NEG)
        mx = jnp.maximum(jnp.max(sc, axis=0, keepdims=True), jnp.max(sn, axis=0, keepdims=True))
        pc = jnp.exp(sc - mx)
        pn = jnp.exp(sn - mx)
        den = jnp.sum(pc, axis=0, keepdims=True) + jnp.sum(pn, axis=0, keepdims=True)
        o = (jnp.sum(pc * vc, axis=0, keepdims=True) + jnp.sum(pn * vn, axis=0, keepdims=True)) / den
        outs.append(o[0])
        lses.append((mx + jnp.log(den))[0])
    lmx = jnp.maximum(jnp.maximum(lses[0], lses[1]), lses[2])
    es = [jnp.exp(l - lmx) for l in lses]
    tot = es[0] + es[1] + es[2]
    o_ref[...] = (outs[0] * (es[0] / tot) + outs[1] * (es[1] / tot) + outs[2] * (es[2] / tot))


def _attn_sample(q, k_new, v_new, caches, nbatch, t_new):
    shp = (nbatch, t_new, N_GROUPS, N_HEADS, HEAD_DIM)
    cviews, cspecs = [], []
    for (win, dil), c in zip(GROUPS, caches):
        n_c = win // dil
        cviews.append(c.reshape(nbatch, n_c, dil, 2, N_HEADS, HEAD_DIM))
        cspecs.append(pl.BlockSpec((None, n_c, None, 2, N_HEADS, HEAD_DIM),
                                   lambda b, t, dil=dil: (b, 0, lax.rem(t, dil), 0, 0, 0)))
    new_spec = pl.BlockSpec((None, t_new, N_GROUPS, N_HEADS, HEAD_DIM), lambda b, t: (b, 0, 0, 0, 0))
    kern = functools.partial(_attn_sample_kernel, t_new=t_new)
    o = pl.pallas_call(
        kern,
        out_shape=jax.ShapeDtypeStruct((nbatch, t_new, N_HEADS, HEAD_DIM), F32),
        grid=(nbatch, t_new),
        in_specs=[
            pl.BlockSpec((None, None, N_GROUPS, N_HEADS, HEAD_DIM), lambda b, t: (b, t, 0, 0, 0)),
            new_spec, new_spec, *cspecs,
        ],
        out_specs=pl.BlockSpec((None, None, N_HEADS, HEAD_DIM), lambda b, t: (b, t, 0, 0)),
        compiler_params=_params(2),
        name="attn_sample",
    )(q.reshape(shp), k_new.reshape(shp), v_new.reshape(shp), *cviews)
    return o.reshape(nbatch * t_new, N_HEADS * HEAD_DIM)


def _trunk(x, pos, batch, seq, conv_state, kv_caches, wts):
    m = x.shape[0]
    tm = min(512, m)
    tabs = _rope_tables(pos)
    n_a = wts["conv_w_in"].shape[0]
    depth = wts["ffn_w_in"].shape[0]
    row = lambda v: v.reshape(1, -1)
    zero_bias = jnp.zeros((1, D_MODEL), F32)
    new_conv = []
    k = v = None
    for l in range(depth):
        if l < n_a:
            g = _glu(x, row(wts["norm_mix"][l]), wts["conv_w_in"], row(wts["conv_b_in"][l]), l, tm)
            w_taps = jnp.pad(wts["conv_w_dw"][l],
                             ((CONV_HALO - CONV_STATE, CONV_TAPS - CONV_HALO - 1), (0, 0)))
            cargs = (w_taps, row(wts["conv_b_dw"][l]), row(wts["conv_ln_g"][l]), row(wts["conv_ln_b"][l]))
            g3 = g.reshape(batch, seq, D_MODEL)
            if conv_state is None:
                y = _conv_prompt(g, *cargs, seq=seq)
                new_conv.append(g3[:, seq - CONV_STATE:])
            else:
                buf = jnp.concatenate([conv_state[l], g3], axis=1)
                new_conv.append(buf[:, -CONV_STATE:])
                lead = CONV_HALO - CONV_STATE
                buf = jnp.pad(buf, ((0, 0), (lead, CONV_TAPS - lead - CONV_STATE - seq), (0, 0)))
                y = _conv_sample(buf, *cargs, t_new=seq).reshape(m, D_MODEL)
            x = _out_proj(y, wts["conv_w_out"], row(wts["conv_b_out"][l]), x, l, tm)
        else:
            j = l - n_a
            if j == 0:
                gkv = row(wts["norm_kv"])
                k = _proj(x, gkv, wts["w_kv"], 0, 0, QW, tabs, True, tm)
                v = _proj(x, gkv, wts["w_kv"], 0, QW, QW, tabs, False, tm)
            q = _proj(x, row(wts["norm_mix"][l]), wts["attn_w_q"], j, 0, QW, tabs, True, tm)
            if kv_caches is None:
                o = _attn_prompt(q, k, v, batch, seq)
            else:
                o = _attn_sample(q, k, v, kv_caches, batch, seq)
            x = _out_proj(o, wts["attn_w_o"], zero_bias, x, j, tm)
        x = _ffn(x, row(wts["norm_ffn"][l]), wts["ffn_w_in"], wts["ffn_w_out"],
                 row(wts["norm_final"]), l, tm, final_norm=(l == depth - 1))
    shp = (batch, seq, N_GROUPS, N_HEADS, HEAD_DIM)
    kv5 = jnp.stack([k.reshape(shp), v.reshape(shp)], axis=2)
    return x, jnp.stack(new_conv, axis=0), kv5


def kernel(x_prompt, x_sample, state_conv, cache_kv_g0, cache_kv_g1, cache_kv_g2, norm_mix, norm_ffn, conv_w_in, conv_b_in, conv_w_dw, conv_b_dw, conv_ln_g, conv_ln_b, conv_w_out, conv_b_out, norm_kv, w_kv, attn_w_q, attn_w_o, ffn_w_in, ffn_w_out, norm_final):
    wts = dict(norm_mix=norm_mix, norm_ffn=norm_ffn, conv_b_in=conv_b_in, conv_w_dw=conv_w_dw,
               conv_b_dw=conv_b_dw, conv_ln_g=conv_ln_g, conv_ln_b=conv_ln_b, conv_b_out=conv_b_out,
               norm_kv=norm_kv, norm_final=norm_final)
    for name, w in (("conv_w_in", conv_w_in), ("conv_w_out", conv_w_out), ("w_kv", w_kv[None]),
                    ("attn_w_q", attn_w_q), ("attn_w_o", attn_w_o), ("ffn_w_in", ffn_w_in),
                    ("ffn_w_out", ffn_w_out)):
        wts[name] = w.astype(BF16)

    bp, sp, d = x_prompt.shape
    pos_p = jnp.arange(sp, dtype=jnp.int32)
    y_p, conv_p, kv_p = _trunk(x_prompt.reshape(bp * sp, d), pos_p, bp, sp, None, None, wts)
    kv_out_p = [kv_p[:, sp - min(w, sp):, :, g] for g, (w, _) in enumerate(GROUPS)]

    bs, ss, _ = x_sample.shape
    caches = (cache_kv_g0, cache_kv_g1, cache_kv_g2)
    pos_s = jnp.tile(PAST_LEN + jnp.arange(ss, dtype=jnp.int32), bs)
    y_s, conv_s, kv_s = _trunk(x_sample.reshape(bs * ss, d), pos_s, bs, ss, state_conv, caches, wts)
    kv_out_s = [jnp.concatenate([c[:, ss:], kv_s[:, :, :, g]], axis=1) for g, c in enumerate(caches)]

    return (y_p.reshape(bp, sp, d), y_s.reshape(bs, ss, d), conv_p, *kv_out_p, conv_s, *kv_out_s)
```

```python
import functools

import jax
import jax.numpy as jnp
from jax import lax
from jax.experimental import pallas as pl
from jax.experimental.pallas import tpu as pltpu

F32 = jnp.float32
BF16 = jnp.bfloat16

D_MODEL = 2048
N_HEADS = 16
HEAD_DIM = 128
GROUPS = ((128, 1), (512, 4), (2048, 16))
N_GROUPS = len(GROUPS)
PAST_LEN = 16384
QBLK = 128
ATT_SPAN = QBLK * max(d for _, d in GROUPS)
ROT_DIM = HEAD_DIM // 4
ROPE_THETA = 500000.0
CONV_WIDTH = 31
CONV_STATE = CONV_WIDTH - 1
CONV_HALO = 32
CONV_TAPS = 40
RMS_EPS = 1e-6
LN_EPS = 1e-5
NEG = -1e30
SM_SCALE = HEAD_DIM ** -0.5
QW = N_GROUPS * N_HEADS * HEAD_DIM

V7X_VMEM_LIMIT = 56 * 1024 * 1024


def _params(n_axes):
    return pltpu.CompilerParams(dimension_semantics=("arbitrary",) * n_axes,
                                vmem_limit_bytes=V7X_VMEM_LIMIT)


def _rms(x, g):
    return x * lax.rsqrt(jnp.mean(x * x, axis=-1, keepdims=True) + RMS_EPS) * g


def _glu_kernel(x_ref, gn_ref, wa_ref, wg_ref, ba_ref, bg_ref, o_ref, xn_ref):
    @pl.when(pl.program_id(1) == 0)
    def _():
        xn_ref[...] = _rms(x_ref[...], gn_ref[...]).astype(BF16)

    xn = xn_ref[...]
    a = jnp.dot(xn, wa_ref[...], preferred_element_type=F32) + ba_ref[...]
    gate = jnp.dot(xn, wg_ref[...], preferred_element_type=F32) + bg_ref[...]
    o_ref[...] = a * jax.nn.sigmoid(gate)


def _glu(x, gn, w_in, b_in, layer, tm, tn=512):
    m, d = x.shape
    nj = d // tn
    return pl.pallas_call(
        _glu_kernel,
        out_shape=jax.ShapeDtypeStruct((m, d), F32),
        grid=(m // tm, nj),
        in_specs=[
            pl.BlockSpec((tm, d), lambda i, j: (i, 0)),
            pl.BlockSpec((1, d), lambda i, j: (0, 0)),
            pl.BlockSpec((None, d, tn), lambda i, j: (layer, 0, j)),
            pl.BlockSpec((None, d, tn), lambda i, j: (layer, 0, j + nj)),
            pl.BlockSpec((1, tn), lambda i, j: (0, j)),
            pl.BlockSpec((1, tn), lambda i, j: (0, j + nj)),
        ],
        out_specs=pl.BlockSpec((tm, tn), lambda i, j: (i, j)),
        scratch_shapes=[pltpu.VMEM((tm, d), BF16)],
        compiler_params=_params(2),
        name="glu_proj",
    )(x, gn, w_in, w_in, b_in, b_in)


def _conv_taps(buf_ref, w_ref, r0, c0, rows):
    acc = None
    for s in range(8):
        win = rows if s == 0 else rows + 8
        u = None
        for a in range(CONV_TAPS // 8):
            k = 8 * a + s
            if k < CONV_HALO - CONV_STATE or k > CONV_HALO:
                continue
            term = buf_ref[pl.ds(r0 + 8 * a, win), pl.ds(c0, 128)] * w_ref[k:k + 1, pl.ds(c0, 128)]
            u = term if u is None else u + term
        part = u if s == 0 else u[s:s + rows]
        acc = part if acc is None else acc + part
    return acc


def _ln_silu(y, g, b):
    mu = jnp.mean(y, axis=-1, keepdims=True)
    dlt = y - mu
    var = jnp.mean(dlt * dlt, axis=-1, keepdims=True)
    z = dlt * lax.rsqrt(var + LN_EPS) * g + b
    return z * jax.nn.sigmoid(z)


def _conv_prompt_kernel(cur_ref, halo_ref, w_ref, bdw_ref, lg_ref, lb_ref, o_ref,
                        buf_ref, y_ref, *, tiles_per_seq, tm, rc):
    i = pl.program_id(0)
    buf_ref[0:CONV_HALO, :] = jnp.where(i % tiles_per_seq == 0, 0.0, halo_ref[...])
    buf_ref[CONV_HALO:, :] = cur_ref[...]

    def col_body(c, carry):
        c0 = pl.multiple_of(c * 128, 128)

        def row_body(r, carry2):
            r0 = pl.multiple_of(r * rc, rc)
            y_ref[pl.ds(r0, rc), pl.ds(c0, 128)] = _conv_taps(buf_ref, w_ref, r0, c0, rc)
            return carry2

        return lax.fori_loop(0, tm // rc, row_body, carry)

    lax.fori_loop(0, D_MODEL // 128, col_body, 0)
    y = y_ref[...] + bdw_ref[...]
    o_ref[...] = _ln_silu(y, lg_ref[...], lb_ref[...]).astype(o_ref.dtype)


def _conv_prompt(g, w_taps, b_dw, ln_g, ln_b, seq, tm=256, rc=64):
    m, d = g.shape
    hb = tm // CONV_HALO
    kern = functools.partial(_conv_prompt_kernel, tiles_per_seq=seq // tm, tm=tm, rc=rc)
    return pl.pallas_call(
        kern,
        out_shape=jax.ShapeDtypeStruct((m, d), BF16),
        grid=(m // tm,),
        in_specs=[
            pl.BlockSpec((tm, d), lambda i: (i, 0)),
            pl.BlockSpec((CONV_HALO, d), lambda i: (jnp.maximum(i * hb - 1, 0), 0)),
            pl.BlockSpec((CONV_TAPS, d), lambda i: (0, 0)),
            pl.BlockSpec((1, d), lambda i: (0, 0)),
            pl.BlockSpec((1, d), lambda i: (0, 0)),
            pl.BlockSpec((1, d), lambda i: (0, 0)),
        ],
        out_specs=pl.BlockSpec((tm, d), lambda i: (i, 0)),
        scratch_shapes=[pltpu.VMEM((tm + CONV_HALO, d), F32), pltpu.VMEM((tm, d), F32)],
        compiler_params=_params(1),
        name="conv_prompt",
    )(g, g, w_taps, b_dw, ln_g, ln_b)


def _conv_sample_kernel(buf_ref, w_ref, bdw_ref, lg_ref, lb_ref, o_ref, *, t_new):
    acc = None
    for k in range(CONV_HALO - CONV_STATE, CONV_HALO + 1):
        term = buf_ref[k:k + t_new, :] * w_ref[k:k + 1, :]
        acc = term if acc is None else acc + term
    y = acc + bdw_ref[...]
    o_ref[...] = _ln_silu(y, lg_ref[...], lb_ref[...])


def _conv_sample(buf, w_taps, b_dw, ln_g, ln_b, t_new):
    nb, rows, d = buf.shape
    kern = functools.partial(_conv_sample_kernel, t_new=t_new)
    return pl.pallas_call(
        kern,
        out_shape=jax.ShapeDtypeStruct((nb, t_new, d), F32),
        grid=(nb,),
        in_specs=[
            pl.BlockSpec((None, rows, d), lambda b: (b, 0, 0)),
            pl.BlockSpec((CONV_TAPS, d), lambda b: (0, 0)),
            pl.BlockSpec((1, d), lambda b: (0, 0)),
            pl.BlockSpec((1, d), lambda b: (0, 0)),
            pl.BlockSpec((1, d), lambda b: (0, 0)),
        ],
        out_specs=pl.BlockSpec((None, t_new, d), lambda b: (b, 0, 0)),
        compiler_params=_params(1),
        name="conv_sample",
    )(buf, w_taps, b_dw, ln_g, ln_b)


def _out_proj_kernel(y_ref, w_ref, b_ref, x_ref, o_ref):
    acc = jnp.dot(y_ref[...].astype(BF16), w_ref[...], preferred_element_type=F32)
    o_ref[...] = x_ref[...] + acc + b_ref[...]


def _out_proj(y, w, b, x, layer, tm):
    m, d = x.shape
    k = y.shape[1]
    return pl.pallas_call(
        _out_proj_kernel,
        out_shape=jax.ShapeDtypeStruct((m, d), F32),
        grid=(m // tm,),
        in_specs=[
            pl.BlockSpec((tm, k), lambda i: (i, 0)),
            pl.BlockSpec((None, k, d), lambda i: (layer, 0, 0)),
            pl.BlockSpec((1, d), lambda i: (0, 0)),
            pl.BlockSpec((tm, d), lambda i: (i, 0)),
        ],
        out_specs=pl.BlockSpec((tm, d), lambda i: (i, 0)),
        compiler_params=_params(1),
        name="out_proj",
    )(y, w, b, x)


def _ffn_kernel(x_ref, gn_ref, wg_ref, wu_ref, wo_ref, gf_ref, o_ref, xn_ref, *, final_norm):
    f = pl.program_id(1)

    @pl.when(f == 0)
    def _():
        x = x_ref[...]
        xn_ref[...] = _rms(x, gn_ref[...]).astype(BF16)
        o_ref[...] = x

    xn = xn_ref[...]
    gt = jnp.dot(xn, wg_ref[...], preferred_element_type=F32)
    up = jnp.dot(xn, wu_ref[...], preferred_element_type=F32)
    h = (gt * jax.nn.sigmoid(gt) * up).astype(BF16)
    o_ref[...] += jnp.dot(h, wo_ref[...], preferred_element_type=F32)

    if final_norm:
        @pl.when(f == pl.num_programs(1) - 1)
        def _():
            o_ref[...] = _rms(o_ref[...], gf_ref[...])


def _ffn(x, gn, w_in, w_out, gf, layer, tm, final_norm, tf=512):
    m, d = x.shape
    dff = w_out.shape[1]
    nf = dff // tf
    kern = functools.partial(_ffn_kernel, final_norm=final_norm)
    return pl.pallas_call(
        kern,
        out_shape=jax.ShapeDtypeStruct((m, d), F32),
        grid=(m // tm, nf),
        in_specs=[
            pl.BlockSpec((tm, d), lambda i, f: (i, 0)),
            pl.BlockSpec((1, d), lambda i, f: (0, 0)),
            pl.BlockSpec((None, d, tf), lambda i, f: (layer, 0, f)),
            pl.BlockSpec((None, d, tf), lambda i, f: (layer, 0, f + nf)),
            pl.BlockSpec((None, tf, d), lambda i, f: (layer, f, 0)),
            pl.BlockSpec((1, d), lambda i, f: (0, 0)),
        ],
        out_specs=pl.BlockSpec((tm, d), lambda i, f: (i, 0)),
        scratch_shapes=[pltpu.VMEM((tm, d), BF16)],
        compiler_params=_params(2),
        name="ffn",
    )(x, gn, w_in, w_in, w_out, gf)


def _rope(acc, cos_t, sin_a, sin_b):
    outs = []
    for hh in range(acc.shape[1] // HEAD_DIM):
        xs = acc[:, hh * HEAD_DIM:(hh + 1) * HEAD_DIM]
        outs.append(xs * cos_t
                    + pltpu.roll(xs, HEAD_DIM - ROT_DIM // 2, axis=1) * sin_a
                    + pltpu.roll(xs, ROT_DIM // 2, axis=1) * sin_b)
    return jnp.concatenate(outs, axis=1)


def _proj_kernel(x_ref, gn_ref, w_ref, o_ref, xn_ref):
    @pl.when(pl.program_id(1) == 0)
    def _():
        xn_ref[...] = _rms(x_ref[...], gn_ref[...]).astype(BF16)

    o_ref[...] = jnp.dot(xn_ref[...], w_ref[...], preferred_element_type=F32)


def _proj(x, gn, w, layer, col0, n, tm, tn=512):
    m, d = x.shape
    cb0 = col0 // tn
    return pl.pallas_call(
        _proj_kernel,
        out_shape=jax.ShapeDtypeStruct((m, n), F32),
        grid=(m // tm, n // tn),
        in_specs=[
            pl.BlockSpec((tm, d), lambda i, j: (i, 0)),
            pl.BlockSpec((1, d), lambda i, j: (0, 0)),
            pl.BlockSpec((None, d, tn), lambda i, j: (layer, 0, cb0 + j)),
        ],
        out_specs=pl.BlockSpec((tm, tn), lambda i, j: (i, j)),
        scratch_shapes=[pltpu.VMEM((tm, d), BF16)],
        compiler_params=_params(2),
        name="proj",
    )(x, gn, w)


def _proj_rope_kernel(x_ref, gn_ref, w_ref, cos_ref, sa_ref, sb_ref, o_ref, xn_ref, raw_ref, *, nj):
    t = pl.program_id(0)

    @pl.when(t % nj == 0)
    def _():
        xn_ref[...] = _rms(x_ref[...], gn_ref[...]).astype(BF16)

    @pl.when(t == 0)
    def _():
        raw_ref[...] = jnp.zeros_like(raw_ref)

    o_ref[...] = _rope(raw_ref[...], cos_ref[...], sa_ref[...], sb_ref[...])
    raw_ref[...] = jnp.dot(xn_ref[...], w_ref[...], preferred_element_type=F32)


def _proj_rope(x, gn, w, layer, col0, n, tabs, tm, tn=512):
    m, d = x.shape
    cos_t, sin_a, sin_b = tabs
    tab_blocks = cos_t.shape[0] // tm
    cb0 = col0 // tn
    nj = n // tn
    tiles = (m // tm) * nj
    kern = functools.partial(_proj_rope_kernel, nj=nj)
    done = lambda t: jnp.maximum(t - 1, 0)
    tab_spec = pl.BlockSpec((tm, HEAD_DIM), lambda t: ((done(t) // nj) % tab_blocks, 0))
    return pl.pallas_call(
        kern,
        out_shape=jax.ShapeDtypeStruct((m, n), F32),
        grid=(tiles + 1,),
        in_specs=[
            pl.BlockSpec((tm, d), lambda t: (jnp.minimum(t, tiles - 1) // nj, 0)),
            pl.BlockSpec((1, d), lambda t: (0, 0)),
            pl.BlockSpec((None, d, tn), lambda t: (layer, 0, cb0 + t % nj)),
            tab_spec, tab_spec, tab_spec,
        ],
        out_specs=pl.BlockSpec((tm, tn), lambda t: (done(t) // nj, done(t) % nj)),
        scratch_shapes=[pltpu.VMEM((tm, d), BF16), pltpu.VMEM((tm, tn), F32)],
        compiler_params=_params(1),
        name="proj_rope",
    )(x, gn, w, cos_t, sin_a, sin_b)


def _rope_tables(pos):
    half = ROT_DIM // 2
    inv = ROPE_THETA ** (-jnp.arange(half, dtype=F32) / half)
    ang = pos.astype(F32)[:, None] * inv[None, :]
    cos, sin = jnp.cos(ang), jnp.sin(ang)
    n = pos.shape[0]
    ones = jnp.ones((n, HEAD_DIM - ROT_DIM), F32)
    zeros = jnp.zeros((n, HEAD_DIM - ROT_DIM), F32)
    zh = jnp.zeros((n, half), F32)
    cos_t = jnp.concatenate([cos, cos, ones], axis=1)
    sin_a = jnp.concatenate([-sin, zh, zeros], axis=1)
    sin_b = jnp.concatenate([zh, sin, zeros], axis=1)
    return cos_t, sin_a, sin_b


def _attn_prompt_kernel(q0, q1, q2, k0, k1, k2, v0, v1, v2, o_ref,
                        s_scr, m_scr, p_scr, num_scr, den_scr, mx_scr, kcar_scr, vcar_scr):
    has_prev = pl.program_id(2) > 0

    @pl.when(jnp.logical_not(has_prev))
    def _():
        kcar_scr[...] = jnp.zeros_like(kcar_scr)
        vcar_scr[...] = jnp.zeros_like(vcar_scr)

    qi = lax.broadcasted_iota(jnp.int32, (QBLK, QBLK), 0)
    ki = lax.broadcasted_iota(jnp.int32, (QBLK, QBLK), 1)
    mask_c = ki <= qi
    mask_p = ki >= qi
    mask_p0 = jnp.logical_and(mask_p, has_prev)
    ones = jnp.ones((QBLK, HEAD_DIM), BF16)
    nt = (((1,), (1,)), ((), ()))
    refs = ((q0, k0, v0), (q1, k1, v1), (q2, k2, v2))
    car0 = 0

    for g, (_, dil) in enumerate(GROUPS):
        q_ref, kc_ref, vc_ref = refs[g]
        nblk = ATT_SPAN // (QBLK * dil)
        units = [(jb, r) for jb in range(nblk) for r in range(dil)]

        def rows(ref, jb, r, dil=dil):
            start = QBLK * jb * dil + r
            if dil == 1:
                return ref[pl.ds(start, QBLK), :]
            return ref[pl.ds(start, QBLK, stride=dil), :]

        def put(scr, jb, r, val, g=g, dil=dil):
            start = QBLK * jb * dil + r
            if dil == 1:
                scr[g, pl.ds(start, QBLK), :] = val
            else:
                scr[g, pl.ds(start, QBLK, stride=dil), :] = val

        kb = {}
        for u, (jb, r) in enumerate(units):
            q = rows(q_ref, jb, r).astype(BF16)
            kc = kb[jb, r] = rows(kc_ref, jb, r).astype(BF16)
            if jb > 0:
                kp, mp = kb.pop((jb - 1, r)), mask_p
            else:
                kp, mp = kcar_scr[car0 + r], mask_p0
            if jb == nblk - 1:
                kcar_scr[car0 + r] = kc
            sc = lax.dot_general(q, kc, nt, preferred_element_type=F32) * SM_SCALE
            sp = lax.dot_general(q, kp, nt, preferred_element_type=F32) * SM_SCALE
            sc = jnp.where(mask_c, sc, NEG)
            sp = jnp.where(mp, sp, NEG)
            s_scr[u, :, 0:QBLK] = sp
            s_scr[u, :, QBLK:] = sc
            mx = jnp.max(jnp.maximum(sp, sc), axis=-1, keepdims=True)
            m_scr[u] = jnp.broadcast_to(mx, (QBLK, HEAD_DIM))

        for u in range(len(units)):
            mx = m_scr[u]
            p_scr[u, :, 0:QBLK] = jnp.exp(s_scr[u, :, 0:QBLK] - mx).astype(BF16)
            p_scr[u, :, QBLK:] = jnp.exp(s_scr[u, :, QBLK:] - mx).astype(BF16)

        vb = {}
        for u, (jb, r) in enumerate(units):
            vc = vb[jb, r] = rows(vc_ref, jb, r).astype(BF16)
            vp = vb.pop((jb - 1, r)) if jb > 0 else vcar_scr[car0 + r]
            if jb == nblk - 1:
                vcar_scr[car0 + r] = vc
            acc = (jnp.dot(p_scr[u, :, 0:QBLK], jnp.concatenate([vp, ones], axis=1),
                           preferred_element_type=F32)
                   + jnp.dot(p_scr[u, :, QBLK:], jnp.concatenate([vc, ones], axis=1),
                             preferred_element_type=F32))
            put(num_scr, jb, r, acc[:, 0:HEAD_DIM])
            put(den_scr, jb, r, acc[:, HEAD_DIM:])
            put(mx_scr, jb, r, m_scr[u])
        car0 += dil

    ch = 256
    for c in range(ATT_SPAN // ch):
        rs = pl.ds(c * ch, ch)
        m0, m1, m2 = mx_scr[0, rs, :], mx_scr[1, rs, :], mx_scr[2, rs, :]
        mm = jnp.maximum(jnp.maximum(m0, m1), m2)
        a0, a1, a2 = jnp.exp(m0 - mm), jnp.exp(m1 - mm), jnp.exp(m2 - mm)
        num = a0 * num_scr[0, rs, :] + a1 * num_scr[1, rs, :] + a2 * num_scr[2, rs, :]
        den = a0 * den_scr[0, rs, :] + a1 * den_scr[1, rs, :] + a2 * den_scr[2, rs, :]
        o_ref[rs, :] = (num / den).astype(o_ref.dtype)


def _attn_prompt(q, k, v, batch, seq):
    m = q.shape[0]
    assert seq % ATT_SPAN == 0
    nspan = seq // ATT_SPAN

    def cur(g):
        return pl.BlockSpec((ATT_SPAN, HEAD_DIM), lambda b, h, s: (b * nspan + s, g * N_HEADS + h))

    gs = range(N_GROUPS)
    n_carry = sum(d for _, d in GROUPS)
    acc = lambda: pltpu.VMEM((N_GROUPS, ATT_SPAN, HEAD_DIM), F32)
    car = lambda: pltpu.VMEM((n_carry, QBLK, HEAD_DIM), BF16)
    return pl.pallas_call(
        _attn_prompt_kernel,
        out_shape=jax.ShapeDtypeStruct((m, N_HEADS * HEAD_DIM), BF16),
        grid=(batch, N_HEADS, nspan),
        in_specs=[cur(g) for g in gs] * 3,
        out_specs=pl.BlockSpec((ATT_SPAN, HEAD_DIM), lambda b, h, s: (b * nspan + s, h)),
        scratch_shapes=[
            pltpu.VMEM((N_HEADS, QBLK, 2 * QBLK), F32),
            pltpu.VMEM((N_HEADS, QBLK, HEAD_DIM), F32),
            pltpu.VMEM((N_HEADS, QBLK, 2 * QBLK), BF16),
            acc(), acc(), acc(), car(), car(),
        ],
        compiler_params=_params(3),
        name="attn_prompt",
    )(q, q, q, k, k, k, v, v, v)


def _attn_sample_kernel(q_ref, kn_ref, vn_ref, c0_ref, c1_ref, c2_ref, o_ref, *, t_new):
    t = pl.program_id(1)
    caches = (c0_ref, c1_ref, c2_ref)
    outs, lses = [], []
    for g, (win, dil) in enumerate(GROUPS):
        c_ref = caches[g]
        n_c = win // dil
        res = t & (dil - 1)
        qg = q_ref[g]
        kc, vc = c_ref[:, 0], c_ref[:, 1]
        kn, vn = kn_ref[:, g], vn_ref[:, g]
        sc = jnp.sum(kc * qg[None], axis=-1, keepdims=True) * SM_SCALE
        sn = jnp.sum(kn * qg[None], axis=-1, keepdims=True) * SM_SCALE
        mi = lax.broadcasted_iota(jnp.int32, (n_c, N_HEADS, 1), 0)
        sc = jnp.where(mi * dil + res >= t, sc, NEG)
        ti = lax.broadcasted_iota(jnp.int32, (t_new, N_HEADS, 1), 0)
        vis = jnp.logical_and(ti <= t, ((t - ti) & (dil - 1)) == 0)
        sn = jnp.where(vis, sn, NEG)
        mx = jnp.maximum(jnp.max(sc, axis=0, keepdims=True), jnp.max(sn, axis=0, keepdims=True))
        pc = jnp.exp(sc - mx)
        pn = jnp.exp(sn - mx)
        den = jnp.sum(pc, axis=0, keepdims=True) + jnp.sum(pn, axis=0, keepdims=True)
        o = (jnp.sum(pc * vc, axis=0, keepdims=True) + jnp.sum(pn * vn, axis=0, keepdims=True)) / den
        outs.append(o[0])
        lses.append((mx + jnp.log(den))[0])
    lmx = jnp.maximum(jnp.maximum(lses[0], lses[1]), lses[2])
    es = [jnp.exp(l - lmx) for l in lses]
    tot = es[0] + es[1] + es[2]
    o_ref[...] = (outs[0] * (es[0] / tot) + outs[1] * (es[1] / tot) + outs[2] * (es[2] / tot))


def _attn_sample(q, k_new, v_new, caches, nbatch, t_new):
    shp = (nbatch, t_new, N_GROUPS, N_HEADS, HEAD_DIM)
    cviews, cspecs = [], []
    for (win, dil), c in zip(GROUPS, caches):
        n_c = win // dil
        cviews.append(c.reshape(nbatch, n_c, dil, 2, N_HEADS, HEAD_DIM))
        cspecs.append(pl.BlockSpec((None, n_c, None, 2, N_HEADS, HEAD_DIM),
                                   lambda b, t, dil=dil: (b, 0, lax.rem(t, dil), 0, 0, 0)))
    new_spec = pl.BlockSpec((None, t_new, N_GROUPS, N_HEADS, HEAD_DIM), lambda b, t: (b, 0, 0, 0, 0))
    kern = functools.partial(_attn_sample_kernel, t_new=t_new)
    o = pl.pallas_call(
        kern,
        out_shape=jax.ShapeDtypeStruct((nbatch, t_new, N_HEADS, HEAD_DIM), F32),
        grid=(nbatch, t_new),
        in_specs=[
            pl.BlockSpec((None, None, N_GROUPS, N_HEADS, HEAD_DIM), lambda b, t: (b, t, 0, 0, 0)),
            new_spec, new_spec, *cspecs,
        ],
        out_specs=pl.BlockSpec((None, None, N_HEADS, HEAD_DIM), lambda b, t: (b, t, 0, 0)),
        compiler_params=_params(2),
        name="attn_sample",
    )(q.reshape(shp), k_new.reshape(shp), v_new.reshape(shp), *cviews)
    return o.reshape(nbatch * t_new, N_HEADS * HEAD_DIM)


def _trunk(x, pos, batch, seq, conv_state, kv_caches, wts):
    m = x.shape[0]
    tm = min(512, m)
    tm_wide = min(1024, m)
    tabs = _rope_tables(pos)
    n_a = wts["conv_w_in"].shape[0]
    depth = wts["ffn_w_in"].shape[0]
    row = lambda v: v.reshape(1, -1)
    zero_bias = jnp.zeros((1, D_MODEL), F32)
    new_conv = []
    k = v = None
    for l in range(depth):
        if l < n_a:
            g = _glu(x, row(wts["norm_mix"][l]), wts["conv_w_in"], row(wts["conv_b_in"][l]), l, tm_wide)
            w_taps = jnp.pad(wts["conv_w_dw"][l],
                             ((CONV_HALO - CONV_STATE, CONV_TAPS - CONV_HALO - 1), (0, 0)))
            cargs = (w_taps, row(wts["conv_b_dw"][l]), row(wts["conv_ln_g"][l]), row(wts["conv_ln_b"][l]))
            g3 = g.reshape(batch, seq, D_MODEL)
            if conv_state is None:
                y = _conv_prompt(g, *cargs, seq=seq)
                new_conv.append(g3[:, seq - CONV_STATE:])
            else:
                buf = jnp.concatenate([conv_state[l], g3], axis=1)
                new_conv.append(buf[:, -CONV_STATE:])
                lead = CONV_HALO - CONV_STATE
                buf = jnp.pad(buf, ((0, 0), (lead, CONV_TAPS - lead - CONV_STATE - seq), (0, 0)))
                y = _conv_sample(buf, *cargs, t_new=seq).reshape(m, D_MODEL)
            x = _out_proj(y, wts["conv_w_out"], row(wts["conv_b_out"][l]), x, l, tm)
        else:
            j = l - n_a
            if j == 0:
                gkv = row(wts["norm_kv"])
                k = _proj_rope(x, gkv, wts["w_kv"], 0, 0, QW, tabs, tm_wide)
                v = _proj(x, gkv, wts["w_kv"], 0, QW, QW, tm_wide)
            q = _proj_rope(x, row(wts["norm_mix"][l]), wts["attn_w_q"], j, 0, QW, tabs, tm_wide)
            if kv_caches is None:
                o = _attn_prompt(q, k, v, batch, seq)
            else:
                o = _attn_sample(q, k, v, kv_caches, batch, seq)
            x = _out_proj(o, wts["attn_w_o"], zero_bias, x, j, tm)
        x = _ffn(x, row(wts["norm_ffn"][l]), wts["ffn_w_in"], wts["ffn_w_out"],
                 row(wts["norm_final"]), l, tm, final_norm=(l == depth - 1))
    return x, jnp.stack(new_conv, axis=0), k.reshape(batch, seq, QW), v.reshape(batch, seq, QW)


def _kv_tail(k3, v3, g, rows):
    gw = N_HEADS * HEAD_DIM
    t = k3.shape[1]
    shp = (k3.shape[0], rows, N_HEADS, HEAD_DIM)
    return jnp.stack([a[:, t - rows:, g * gw:(g + 1) * gw].reshape(shp) for a in (k3, v3)], axis=2)


def kernel(x_prompt, x_sample, state_conv, cache_kv_g0, cache_kv_g1, cache_kv_g2, norm_mix, norm_ffn, conv_w_in, conv_b_in, conv_w_dw, conv_b_dw, conv_ln_g, conv_ln_b, conv_w_out, conv_b_out, norm_kv, w_kv, attn_w_q, attn_w_o, ffn_w_in, ffn_w_out, norm_final):
    wts = dict(norm_mix=norm_mix, norm_ffn=norm_ffn, conv_b_in=conv_b_in, conv_w_dw=conv_w_dw,
               conv_b_dw=conv_b_dw, conv_ln_g=conv_ln_g, conv_ln_b=conv_ln_b, conv_b_out=conv_b_out,
               norm_kv=norm_kv, norm_final=norm_final)
    for name, w in (("conv_w_in", conv_w_in), ("conv_w_out", conv_w_out), ("w_kv", w_kv[None]),
                    ("attn_w_q", attn_w_q), ("attn_w_o", attn_w_o), ("ffn_w_in", ffn_w_in),
                    ("ffn_w_out", ffn_w_out)):
        wts[name] = w.astype(BF16)

    bp, sp, d = x_prompt.shape
    pos_p = jnp.arange(sp, dtype=jnp.int32)
    y_p, conv_p, k_p, v_p = _trunk(x_prompt.reshape(bp * sp, d), pos_p, bp, sp, None, None, wts)
    kv_out_p = [_kv_tail(k_p, v_p, g, min(w, sp)) for g, (w, _) in enumerate(GROUPS)]

    bs, ss, _ = x_sample.shape
    caches = (cache_kv_g0, cache_kv_g1, cache_kv_g2)
    pos_s = jnp.tile(PAST_LEN + jnp.arange(ss, dtype=jnp.int32), bs)
    y_s, conv_s, k_s, v_s = _trunk(x_sample.reshape(bs * ss, d), pos_s, bs, ss, state_conv, caches, wts)
    kv_out_s = [jnp.concatenate([c[:, ss:], _kv_tail(k_s, v_s, g, ss)], axis=1)
                for g, c in enumerate(caches)]

    return (y_p.reshape(bp, sp, d), y_s.reshape(bs, ss, d), conv_p, *kv_out_p, conv_s, *kv_out_s)
```

```python
import functools

import jax
import jax.numpy as jnp
from jax import lax
from jax.experimental import pallas as pl
from jax.experimental.pallas import tpu as pltpu

F32 = jnp.float32
BF16 = jnp.bfloat16

D_MODEL = 2048
N_HEADS = 16
HEAD_DIM = 128
GROUPS = ((128, 1), (512, 4), (2048, 16))
N_GROUPS = len(GROUPS)
PAST_LEN = 16384
QBLK = 128
ATT_SPAN = QBLK * max(d for _, d in GROUPS)
ROT_DIM = HEAD_DIM // 4
ROPE_THETA = 500000.0
CONV_WIDTH = 31
CONV_STATE = CONV_WIDTH - 1
CONV_HALO = 32
CONV_TAPS = 40
RMS_EPS = 1e-6
LN_EPS = 1e-5
NEG = -1e30
SM_SCALE = HEAD_DIM ** -0.5
QW = N_GROUPS * N_HEADS * HEAD_DIM

V7X_VMEM_LIMIT = 56 * 1024 * 1024


def _params(n_axes):
    return pltpu.CompilerParams(dimension_semantics=("arbitrary",) * n_axes,
                                vmem_limit_bytes=V7X_VMEM_LIMIT)


def _rms(x, g):
    return x * lax.rsqrt(jnp.mean(x * x, axis=-1, keepdims=True) + RMS_EPS) * g


def _glu_kernel(x_ref, gn_ref, wa_ref, wg_ref, ba_ref, bg_ref, o_ref, xn_ref):
    @pl.when(pl.program_id(1) == 0)
    def _():
        xn_ref[...] = _rms(x_ref[...], gn_ref[...]).astype(BF16)

    xn = xn_ref[...]
    a = jnp.dot(xn, wa_ref[...], preferred_element_type=F32) + ba_ref[...]
    gate = jnp.dot(xn, wg_ref[...], preferred_element_type=F32) + bg_ref[...]
    o_ref[...] = a * jax.nn.sigmoid(gate)


def _glu(x, gn, w_in, b_in, layer, tm, tn=512):
    m, d = x.shape
    nj = d // tn
    return pl.pallas_call(
        _glu_kernel,
        out_shape=jax.ShapeDtypeStruct((m, d), F32),
        grid=(m // tm, nj),
        in_specs=[
            pl.BlockSpec((tm, d), lambda i, j: (i, 0)),
            pl.BlockSpec((1, d), lambda i, j: (0, 0)),
            pl.BlockSpec((None, d, tn), lambda i, j: (layer, 0, j)),
            pl.BlockSpec((None, d, tn), lambda i, j: (layer, 0, j + nj)),
            pl.BlockSpec((1, tn), lambda i, j: (0, j)),
            pl.BlockSpec((1, tn), lambda i, j: (0, j + nj)),
        ],
        out_specs=pl.BlockSpec((tm, tn), lambda i, j: (i, j)),
        scratch_shapes=[pltpu.VMEM((tm, d), BF16)],
        compiler_params=_params(2),
        name="glu_proj",
    )(x, gn, w_in, w_in, b_in, b_in)


def _conv_taps(buf_ref, w_ref, r0, c0, rows):
    acc = None
    for s in range(8):
        win = rows if s == 0 else rows + 8
        u = None
        for a in range(CONV_TAPS // 8):
            k = 8 * a + s
            if k < CONV_HALO - CONV_STATE or k > CONV_HALO:
                continue
            term = buf_ref[pl.ds(r0 + 8 * a, win), pl.ds(c0, 128)] * w_ref[k:k + 1, pl.ds(c0, 128)]
            u = term if u is None else u + term
        part = u if s == 0 else u[s:s + rows]
        acc = part if acc is None else acc + part
    return acc


def _ln_silu(y, g, b):
    mu = jnp.mean(y, axis=-1, keepdims=True)
    dlt = y - mu
    var = jnp.mean(dlt * dlt, axis=-1, keepdims=True)
    z = dlt * lax.rsqrt(var + LN_EPS) * g + b
    return z * jax.nn.sigmoid(z)


def _conv_prompt_kernel(cur_ref, halo_ref, w_ref, bdw_ref, lg_ref, lb_ref, o_ref,
                        buf_ref, y_ref, *, tiles_per_seq, tm, rc):
    i = pl.program_id(0)
    buf_ref[0:CONV_HALO, :] = jnp.where(i % tiles_per_seq == 0, 0.0, halo_ref[...])
    buf_ref[CONV_HALO:, :] = cur_ref[...]

    def col_body(c, carry):
        c0 = pl.multiple_of(c * 128, 128)

        def row_body(r, carry2):
            r0 = pl.multiple_of(r * rc, rc)
            y_ref[pl.ds(r0, rc), pl.ds(c0, 128)] = _conv_taps(buf_ref, w_ref, r0, c0, rc)
            return carry2

        return lax.fori_loop(0, tm // rc, row_body, carry)

    lax.fori_loop(0, D_MODEL // 128, col_body, 0)
    y = y_ref[...] + bdw_ref[...]
    o_ref[...] = _ln_silu(y, lg_ref[...], lb_ref[...]).astype(o_ref.dtype)


def _conv_prompt(g, w_taps, b_dw, ln_g, ln_b, seq, tm=256, rc=64):
    m, d = g.shape
    hb = tm // CONV_HALO
    kern = functools.partial(_conv_prompt_kernel, tiles_per_seq=seq // tm, tm=tm, rc=rc)
    return pl.pallas_call(
        kern,
        out_shape=jax.ShapeDtypeStruct((m, d), BF16),
        grid=(m // tm,),
        in_specs=[
            pl.BlockSpec((tm, d), lambda i: (i, 0)),
            pl.BlockSpec((CONV_HALO, d), lambda i: (jnp.maximum(i * hb - 1, 0), 0)),
            pl.BlockSpec((CONV_TAPS, d), lambda i: (0, 0)),
            pl.BlockSpec((1, d), lambda i: (0, 0)),
            pl.BlockSpec((1, d), lambda i: (0, 0)),
            pl.BlockSpec((1, d), lambda i: (0, 0)),
        ],
        out_specs=pl.BlockSpec((tm, d), lambda i: (i, 0)),
        scratch_shapes=[pltpu.VMEM((tm + CONV_HALO, d), F32), pltpu.VMEM((tm, d), F32)],
        compiler_params=_params(1),
        name="conv_prompt",
    )(g, g, w_taps, b_dw, ln_g, ln_b)


def _conv_sample_kernel(buf_ref, w_ref, bdw_ref, lg_ref, lb_ref, o_ref, *, t_new):
    acc = None
    for k in range(CONV_HALO - CONV_STATE, CONV_HALO + 1):
        term = buf_ref[k:k + t_new, :] * w_ref[k:k + 1, :]
        acc = term if acc is None else acc + term
    y = acc + bdw_ref[...]
    o_ref[...] = _ln_silu(y, lg_ref[...], lb_ref[...])


def _conv_sample(buf, w_taps, b_dw, ln_g, ln_b, t_new):
    nb, rows, d = buf.shape
    kern = functools.partial(_conv_sample_kernel, t_new=t_new)
    return pl.pallas_call(
        kern,
        out_shape=jax.ShapeDtypeStruct((nb, t_new, d), F32),
        grid=(nb,),
        in_specs=[
            pl.BlockSpec((None, rows, d), lambda b: (b, 0, 0)),
            pl.BlockSpec((CONV_TAPS, d), lambda b: (0, 0)),
            pl.BlockSpec((1, d), lambda b: (0, 0)),
            pl.BlockSpec((1, d), lambda b: (0, 0)),
            pl.BlockSpec((1, d), lambda b: (0, 0)),
        ],
        out_specs=pl.BlockSpec((None, t_new, d), lambda b: (b, 0, 0)),
        compiler_params=_params(1),
        name="conv_sample",
    )(buf, w_taps, b_dw, ln_g, ln_b)


def _out_proj_kernel(y_ref, w_ref, b_ref, x_ref, o_ref):
    acc = jnp.dot(y_ref[...].astype(BF16), w_ref[...], preferred_element_type=F32)
    o_ref[...] = x_ref[...] + acc + b_ref[...]


def _out_proj(y, w, b, x, layer, tm):
    m, d = x.shape
    k = y.shape[1]
    return pl.pallas_call(
        _out_proj_kernel,
        out_shape=jax.ShapeDtypeStruct((m, d), F32),
        grid=(m // tm,),
        in_specs=[
            pl.BlockSpec((tm, k), lambda i: (i, 0)),
            pl.BlockSpec((None, k, d), lambda i: (layer, 0, 0)),
            pl.BlockSpec((1, d), lambda i: (0, 0)),
            pl.BlockSpec((tm, d), lambda i: (i, 0)),
        ],
        out_specs=pl.BlockSpec((tm, d), lambda i: (i, 0)),
        compiler_params=_params(1),
        name="out_proj",
    )(y, w, b, x)


def _ffn_kernel(x_ref, gn_ref, wg_ref, wu_ref, wo_ref, gf_ref, o_ref, xn_ref, *, final_norm):
    f = pl.program_id(1)

    @pl.when(f == 0)
    def _():
        x = x_ref[...]
        xn_ref[...] = _rms(x, gn_ref[...]).astype(BF16)
        o_ref[...] = x

    xn = xn_ref[...]
    gt = jnp.dot(xn, wg_ref[...], preferred_element_type=F32)
    up = jnp.dot(xn, wu_ref[...], preferred_element_type=F32)
    h = (gt * jax.nn.sigmoid(gt) * up).astype(BF16)
    o_ref[...] += jnp.dot(h, wo_ref[...], preferred_element_type=F32)

    if final_norm:
        @pl.when(f == pl.num_programs(1) - 1)
        def _():
            o_ref[...] = _rms(o_ref[...], gf_ref[...])


def _ffn(x, gn, w_in, w_out, gf, layer, tm, final_norm, tf=512):
    m, d = x.shape
    dff = w_out.shape[1]
    nf = dff // tf
    kern = functools.partial(_ffn_kernel, final_norm=final_norm)
    return pl.pallas_call(
        kern,
        out_shape=jax.ShapeDtypeStruct((m, d), F32),
        grid=(m // tm, nf),
        in_specs=[
            pl.BlockSpec((tm, d), lambda i, f: (i, 0)),
            pl.BlockSpec((1, d), lambda i, f: (0, 0)),
            pl.BlockSpec((None, d, tf), lambda i, f: (layer, 0, f)),
            pl.BlockSpec((None, d, tf), lambda i, f: (layer, 0, f + nf)),
            pl.BlockSpec((None, tf, d), lambda i, f: (layer, f, 0)),
            pl.BlockSpec((1, d), lambda i, f: (0, 0)),
        ],
        out_specs=pl.BlockSpec((tm, d), lambda i, f: (i, 0)),
        scratch_shapes=[pltpu.VMEM((tm, d), BF16)],
        compiler_params=_params(2),
        name="ffn",
    )(x, gn, w_in, w_in, w_out, gf)


def _rope(acc, cos_t, sin_a, sin_b):
    outs = []
    for hh in range(acc.shape[1] // HEAD_DIM):
        xs = acc[:, hh * HEAD_DIM:(hh + 1) * HEAD_DIM]
        outs.append(xs * cos_t
                    + pltpu.roll(xs, HEAD_DIM - ROT_DIM // 2, axis=1) * sin_a
                    + pltpu.roll(xs, ROT_DIM // 2, axis=1) * sin_b)
    return jnp.concatenate(outs, axis=1)


def _proj_kernel(x_ref, gn_ref, w_ref, o_ref, xn_ref):
    @pl.when(pl.program_id(1) == 0)
    def _():
        xn_ref[...] = _rms(x_ref[...], gn_ref[...]).astype(BF16)

    o_ref[...] = jnp.dot(xn_ref[...], w_ref[...], preferred_element_type=F32)


def _proj(x, gn, w, layer, col0, n, tm, tn=512):
    m, d = x.shape
    cb0 = col0 // tn
    return pl.pallas_call(
        _proj_kernel,
        out_shape=jax.ShapeDtypeStruct((m, n), F32),
        grid=(m // tm, n // tn),
        in_specs=[
            pl.BlockSpec((tm, d), lambda i, j: (i, 0)),
            pl.BlockSpec((1, d), lambda i, j: (0, 0)),
            pl.BlockSpec((None, d, tn), lambda i, j: (layer, 0, cb0 + j)),
        ],
        out_specs=pl.BlockSpec((tm, tn), lambda i, j: (i, j)),
        scratch_shapes=[pltpu.VMEM((tm, d), BF16)],
        compiler_params=_params(2),
        name="proj",
    )(x, gn, w)


def _proj_rope_kernel(x_ref, gn_ref, w_ref, cos_ref, sa_ref, sb_ref, o_ref, xn_ref, raw_ref, *, nj):
    t = pl.program_id(0)

    @pl.when(t % nj == 0)
    def _():
        xn_ref[...] = _rms(x_ref[...], gn_ref[...]).astype(BF16)

    @pl.when(t == 0)
    def _():
        raw_ref[...] = jnp.zeros_like(raw_ref)

    o_ref[...] = _rope(raw_ref[...], cos_ref[...], sa_ref[...], sb_ref[...])
    raw_ref[...] = jnp.dot(xn_ref[...], w_ref[...], preferred_element_type=F32)


def _proj_rope(x, gn, w, layer, col0, n, tabs, tm, tn=512):
    m, d = x.shape
    cos_t, sin_a, sin_b = tabs
    tab_blocks = cos_t.shape[0] // tm
    cb0 = col0 // tn
    nj = n // tn
    tiles = (m // tm) * nj
    kern = functools.partial(_proj_rope_kernel, nj=nj)
    done = lambda t: jnp.maximum(t - 1, 0)
    tab_spec = pl.BlockSpec((tm, HEAD_DIM), lambda t: ((done(t) // nj) % tab_blocks, 0))
    return pl.pallas_call(
        kern,
        out_shape=jax.ShapeDtypeStruct((m, n), F32),
        grid=(tiles + 1,),
        in_specs=[
            pl.BlockSpec((tm, d), lambda t: (jnp.minimum(t, tiles - 1) // nj, 0)),
            pl.BlockSpec((1, d), lambda t: (0, 0)),
            pl.BlockSpec((None, d, tn), lambda t: (layer, 0, cb0 + t % nj)),
            tab_spec, tab_spec, tab_spec,
        ],
        out_specs=pl.BlockSpec((tm, tn), lambda t: (done(t) // nj, done(t) % nj)),
        scratch_shapes=[pltpu.VMEM((tm, d), BF16), pltpu.VMEM((tm, tn), F32)],
        compiler_params=_params(1),
        name="proj_rope",
    )(x, gn, w, cos_t, sin_a, sin_b)


def _rope_tables(pos):
    half = ROT_DIM // 2
    inv = ROPE_THETA ** (-jnp.arange(half, dtype=F32) / half)
    ang = pos.astype(F32)[:, None] * inv[None, :]
    cos, sin = jnp.cos(ang), jnp.sin(ang)
    n = pos.shape[0]
    ones = jnp.ones((n, HEAD_DIM - ROT_DIM), F32)
    zeros = jnp.zeros((n, HEAD_DIM - ROT_DIM), F32)
    zh = jnp.zeros((n, half), F32)
    cos_t = jnp.concatenate([cos, cos, ones], axis=1)
    sin_a = jnp.concatenate([-sin, zh, zeros], axis=1)
    sin_b = jnp.concatenate([zh, sin, zeros], axis=1)
    return cos_t, sin_a, sin_b


def _attn_prompt_kernel(q0, q1, q2, k0, k1, k2, v0, v1, v2, o_ref,
                        s_scr, m_scr, p_scr, num_scr, den_scr, mx_scr, kcar_scr, vcar_scr):
    has_prev = pl.program_id(2) > 0

    @pl.when(jnp.logical_not(has_prev))
    def _():
        kcar_scr[...] = jnp.zeros_like(kcar_scr)
        vcar_scr[...] = jnp.zeros_like(vcar_scr)

    qi = lax.broadcasted_iota(jnp.int32, (QBLK, QBLK), 0)
    ki = lax.broadcasted_iota(jnp.int32, (QBLK, QBLK), 1)
    mask_c = ki <= qi
    mask_p = ki >= qi
    mask_p0 = jnp.logical_and(mask_p, has_prev)
    ones = jnp.ones((QBLK, HEAD_DIM), BF16)
    nt = (((1,), (1,)), ((), ()))
    refs = ((q0, k0, v0), (q1, k1, v1), (q2, k2, v2))
    car0 = 0

    for g, (_, dil) in enumerate(GROUPS):
        q_ref, kc_ref, vc_ref = refs[g]
        nblk = ATT_SPAN // (QBLK * dil)
        units = [(jb, r) for jb in range(nblk) for r in range(dil)]

        def rows(ref, jb, r, dil=dil):
            start = QBLK * jb * dil + r
            if dil == 1:
                return ref[pl.ds(start, QBLK), :]
            return ref[pl.ds(start, QBLK, stride=dil), :]

        def put(scr, jb, r, val, g=g, dil=dil):
            start = QBLK * jb * dil + r
            if dil == 1:
                scr[g, pl.ds(start, QBLK), :] = val
            else:
                scr[g, pl.ds(start, QBLK, stride=dil), :] = val

        kb = {}
        for u, (jb, r) in enumerate(units):
            q = rows(q_ref, jb, r).astype(BF16)
            kc = kb[jb, r] = rows(kc_ref, jb, r).astype(BF16)
            if jb > 0:
                kp, mp = kb.pop((jb - 1, r)), mask_p
            else:
                kp, mp = kcar_scr[car0 + r], mask_p0
            if jb == nblk - 1:
                kcar_scr[car0 + r] = kc
            sc = lax.dot_general(q, kc, nt, preferred_element_type=F32) * SM_SCALE
            sp = lax.dot_general(q, kp, nt, preferred_element_type=F32) * SM_SCALE
            sc = jnp.where(mask_c, sc, NEG)
            sp = jnp.where(mp, sp, NEG)
            s_scr[u, :, 0:QBLK] = sp
            s_scr[u, :, QBLK:] = sc
            mx = jnp.max(jnp.maximum(sp, sc), axis=-1, keepdims=True)
            m_scr[u] = jnp.broadcast_to(mx, (QBLK, HEAD_DIM))

        for u in range(len(units)):
            mx = m_scr[u]
            p_scr[u, :, 0:QBLK] = jnp.exp(s_scr[u, :, 0:QBLK] - mx).astype(BF16)
            p_scr[u, :, QBLK:] = jnp.exp(s_scr[u, :, QBLK:] - mx).astype(BF16)

        vb = {}
        for u, (jb, r) in enumerate(units):
            vc = vb[jb, r] = rows(vc_ref, jb, r).astype(BF16)
            vp = vb.pop((jb - 1, r)) if jb > 0 else vcar_scr[car0 + r]
            if jb == nblk - 1:
                vcar_scr[car0 + r] = vc
            acc = (jnp.dot(p_scr[u, :, 0:QBLK], jnp.concatenate([vp, ones], axis=1),
                           preferred_element_type=F32)
                   + jnp.dot(p_scr[u, :, QBLK:], jnp.concatenate([vc, ones], axis=1),
                             preferred_element_type=F32))
            put(num_scr, jb, r, acc[:, 0:HEAD_DIM])
            put(den_scr, jb, r, acc[:, HEAD_DIM:])
            put(mx_scr, jb, r, m_scr[u])
        car0 += dil

    ch = 256
    for c in range(ATT_SPAN // ch):
        rs = pl.ds(c * ch, ch)
        m0, m1, m2 = mx_scr[0, rs, :], mx_scr[1, rs, :], mx_scr[2, rs, :]
        mm = jnp.maximum(jnp.maximum(m0, m1), m2)
        a0, a1, a2 = jnp.exp(m0 - mm), jnp.exp(m1 - mm), jnp.exp(m2 - mm)
        num = a0 * num_scr[0, rs, :] + a1 * num_scr[1, rs, :] + a2 * num_scr[2, rs, :]
        den = a0 * den_scr[0, rs, :] + a1 * den_scr[1, rs, :] + a2 * den_scr[2, rs, :]
        o_ref[rs, :] = (num / den).astype(o_ref.dtype)


def _attn_prompt(q, k, v, batch, seq):
    m = q.shape[0]
    assert seq % ATT_SPAN == 0
    nspan = seq // ATT_SPAN

    def cur(g):
        return pl.BlockSpec((ATT_SPAN, HEAD_DIM), lambda b, h, s: (b * nspan + s, g * N_HEADS + h))

    gs = range(N_GROUPS)
    n_carry = sum(d for _, d in GROUPS)
    acc = lambda: pltpu.VMEM((N_GROUPS, ATT_SPAN, HEAD_DIM), F32)
    car = lambda: pltpu.VMEM((n_carry, QBLK, HEAD_DIM), BF16)
    return pl.pallas_call(
        _attn_prompt_kernel,
        out_shape=jax.ShapeDtypeStruct((m, N_HEADS * HEAD_DIM), BF16),
        grid=(batch, N_HEADS, nspan),
        in_specs=[cur(g) for g in gs] * 3,
        out_specs=pl.BlockSpec((ATT_SPAN, HEAD_DIM), lambda b, h, s: (b * nspan + s, h)),
        scratch_shapes=[
            pltpu.VMEM((N_HEADS, QBLK, 2 * QBLK), F32),
            pltpu.VMEM((N_HEADS, QBLK, HEAD_DIM), F32),
            pltpu.VMEM((N_HEADS, QBLK, 2 * QBLK), BF16),
            acc(), acc(), acc(), car(), car(),
        ],
        compiler_params=_params(3),
        name="attn_prompt",
    )(q, q, q, k, k, k, v, v, v)


def _attn_sample_kernel(q_ref, kn_ref, vn_ref, c0_ref, c1_ref, c2_ref, o_ref, *, t_new):
    t = pl.program_id(1)
    caches = (c0_ref, c1_ref, c2_ref)
    outs, lses = [], []
    for g, (win, dil) in enumerate(GROUPS):
        c_ref = caches[g]
        n_c = win // dil
        res = t & (dil - 1)
        qg = q_ref[g]
        kc, vc = c_ref[:, 0], c_ref[:, 1]
        kn, vn = kn_ref[:, g], vn_ref[:, g]
        sc = jnp.sum(kc * qg[None], axis=-1, keepdims=True) * SM_SCALE
        sn = jnp.sum(kn * qg[None], axis=-1, keepdims=True) * SM_SCALE
        mi = lax.broadcasted_iota(jnp.int32, (n_c, N_HEADS, 1), 0)
        sc = jnp.where(mi * dil + res >= t, sc, NEG)
        ti = lax.broadcasted_iota(jnp.int32, (t_new, N_HEADS, 1), 0)
        vis = jnp.logical_and(ti <= t, ((t - ti) & (dil - 1)) == 0)
        sn = jnp.where(vis, sn, NEG)
        mx = jnp.maximum(jnp.max(sc, axis=0, keepdims=True), jnp.max(sn, axis=0, keepdims=True))
        pc = jnp.exp(sc - mx)
        pn = jnp.exp(sn - mx)
        den = jnp.sum(pc, axis=0, keepdims=True) + jnp.sum(pn, axis=0, keepdims=True)
        o = (jnp.sum(pc * vc, axis=0, keepdims=True) + jnp.sum(pn * vn, axis=0, keepdims=True)) / den
        outs.append(o[0])
        lses.append((mx + jnp.log(den))[0])
    lmx = jnp.maximum(jnp.maximum(lses[0], lses[1]), lses[2])
    es = [jnp.exp(l - lmx) for l in lses]
    tot = es[0] + es[1] + es[2]
    o_ref[...] = (outs[0] * (es[0] / tot) + outs[1] * (es[1] / tot) + outs[2] * (es[2] / tot))


def _attn_sample(q, k_new, v_new, caches, nbatch, t_new):
    shp = (nbatch, t_new, N_GROUPS, N_HEADS, HEAD_DIM)
    cviews, cspecs = [], []
    for (win, dil), c in zip(GROUPS, caches):
        n_c = win // dil
        cviews.append(c.reshape(nbatch, n_c, dil, 2, N_HEADS, HEAD_DIM))
        cspecs.append(pl.BlockSpec((None, n_c, None, 2, N_HEADS, HEAD_DIM),
                                   lambda b, t, dil=dil: (b, 0, lax.rem(t, dil), 0, 0, 0)))
    new_spec = pl.BlockSpec((None, t_new, N_GROUPS, N_HEADS, HEAD_DIM), lambda b, t: (b, 0, 0, 0, 0))
    kern = functools.partial(_attn_sample_kernel, t_new=t_new)
    o = pl.pallas_call(
        kern,
        out_shape=jax.ShapeDtypeStruct((nbatch, t_new, N_HEADS, HEAD_DIM), F32),
        grid=(nbatch, t_new),
        in_specs=[
            pl.BlockSpec((None, None, N_GROUPS, N_HEADS, HEAD_DIM), lambda b, t: (b, t, 0, 0, 0)),
            new_spec, new_spec, *cspecs,
        ],
        out_specs=pl.BlockSpec((None, None, N_HEADS, HEAD_DIM), lambda b, t: (b, t, 0, 0)),
        compiler_params=_params(2),
        name="attn_sample",
    )(q.reshape(shp), k_new.reshape(shp), v_new.reshape(shp), *cviews)
    return o.reshape(nbatch * t_new, N_HEADS * HEAD_DIM)


def _kv_tail_kernel(k_ref, v_ref, o_ref, *, rows):
    for kv, ref in enumerate((k_ref, v_ref)):
        for h in range(N_HEADS):
            o_ref[pl.ds(kv * N_HEADS + h, rows, stride=2 * N_HEADS), :] = (
                ref[:, h * HEAD_DIM:(h + 1) * HEAD_DIM])


def _kv_tail(k, v, g, batch, seq, win, tr=128):
    gw = N_HEADS * HEAD_DIM
    nblk = win // tr
    first = (seq - win) // tr
    per_seq = seq // tr
    in_spec = pl.BlockSpec((tr, gw), lambda b, i: (b * per_seq + first + i, g))
    flat = pl.pallas_call(
        functools.partial(_kv_tail_kernel, rows=tr),
        out_shape=jax.ShapeDtypeStruct((batch * win * 2 * N_HEADS, HEAD_DIM), F32),
        grid=(batch, nblk),
        in_specs=[in_spec, in_spec],
        out_specs=pl.BlockSpec((tr * 2 * N_HEADS, HEAD_DIM), lambda b, i: (b * nblk + i, 0)),
        compiler_params=_params(2),
        name="kv_tail",
    )(k, v)
    return flat.reshape(batch, win, 2, N_HEADS, HEAD_DIM)


def _cache_shift_kernel(a_ref, nxt_ref, new_ref, o_ref, *, rows, t_new):
    o_ref[0:rows - t_new] = a_ref[t_new:rows]
    last = pl.program_id(1) == pl.num_programs(1) - 1
    o_ref[rows - t_new:rows] = jnp.where(last, new_ref[...], nxt_ref[...])


def _cache_shift(cache, new, tr=256):
    nbatch, win = cache.shape[:2]
    t_new = new.shape[1]
    tr = min(tr, win)
    nblk = win // tr
    tail = cache.shape[2:]
    zeros = (0,) * len(tail)
    return pl.pallas_call(
        functools.partial(_cache_shift_kernel, rows=tr, t_new=t_new),
        out_shape=jax.ShapeDtypeStruct(cache.shape, cache.dtype),
        grid=(nbatch, nblk),
        in_specs=[
            pl.BlockSpec((None, tr) + tail, lambda b, i: (b, i) + zeros),
            pl.BlockSpec((None, t_new) + tail,
                         lambda b, i: (b, jnp.minimum(i + 1, nblk - 1) * (tr // t_new)) + zeros),
            pl.BlockSpec((None, t_new) + tail, lambda b, i: (b, 0) + zeros),
        ],
        out_specs=pl.BlockSpec((None, tr) + tail, lambda b, i: (b, i) + zeros),
        compiler_params=_params(2),
        name="cache_shift",
    )(cache, cache, new)


def _trunk(x, pos, batch, seq, conv_state, kv_caches, wts):
    m = x.shape[0]
    tm = min(512, m)
    tm_wide = min(1024, m)
    tabs = _rope_tables(pos)
    n_a = wts["conv_w_in"].shape[0]
    depth = wts["ffn_w_in"].shape[0]
    row = lambda v: v.reshape(1, -1)
    zero_bias = jnp.zeros((1, D_MODEL), F32)
    new_conv = []
    k = v = None
    for l in range(depth):
        if l < n_a:
            w_taps = jnp.pad(wts["conv_w_dw"][l],
                             ((CONV_HALO - CONV_STATE, CONV_TAPS - CONV_HALO - 1), (0, 0)))
            cargs = (w_taps, row(wts["conv_b_dw"][l]), row(wts["conv_ln_g"][l]), row(wts["conv_ln_b"][l]))
            g = _glu(x, row(wts["norm_mix"][l]), wts["conv_w_in"], row(wts["conv_b_in"][l]), l, tm_wide)
            g3 = g.reshape(batch, seq, D_MODEL)
            if conv_state is None:
                y = _conv_prompt(g, *cargs, seq=seq)
                new_conv.append(g3[:, seq - CONV_STATE:])
            else:
                buf = jnp.concatenate([conv_state[l], g3], axis=1)
                new_conv.append(buf[:, -CONV_STATE:])
                lead = CONV_HALO - CONV_STATE
                buf = jnp.pad(buf, ((0, 0), (lead, CONV_TAPS - lead - CONV_STATE - seq), (0, 0)))
                y = _conv_sample(buf, *cargs, t_new=seq).reshape(m, D_MODEL)
            x = _out_proj(y, wts["conv_w_out"], row(wts["conv_b_out"][l]), x, l, tm)
        else:
            j = l - n_a
            if j == 0:
                gkv = row(wts["norm_kv"])
                k = _proj_rope(x, gkv, wts["w_kv"], 0, 0, QW, tabs, tm_wide)
                v = _proj(x, gkv, wts["w_kv"], 0, QW, QW, tm_wide)
            q = _proj_rope(x, row(wts["norm_mix"][l]), wts["attn_w_q"], j, 0, QW, tabs, tm_wide)
            if kv_caches is None:
                o = _attn_prompt(q, k, v, batch, seq)
            else:
                o = _attn_sample(q, k, v, kv_caches, batch, seq)
            x = _out_proj(o, wts["attn_w_o"], zero_bias, x, j, tm)
        x = _ffn(x, row(wts["norm_ffn"][l]), wts["ffn_w_in"], wts["ffn_w_out"],
                 row(wts["norm_final"]), l, tm_wide, final_norm=(l == depth - 1))
    return x, jnp.stack(new_conv, axis=0), k.reshape(batch, seq, QW), v.reshape(batch, seq, QW)


def _kv_new_rows(k3, v3, g):
    gw = N_HEADS * HEAD_DIM
    shp = k3.shape[:2] + (N_HEADS, HEAD_DIM)
    return jnp.stack([a[:, :, g * gw:(g + 1) * gw].reshape(shp) for a in (k3, v3)], axis=2)


def kernel(x_prompt, x_sample, state_conv, cache_kv_g0, cache_kv_g1, cache_kv_g2, norm_mix, norm_ffn, conv_w_in, conv_b_in, conv_w_dw, conv_b_dw, conv_ln_g, conv_ln_b, conv_w_out, conv_b_out, norm_kv, w_kv, attn_w_q, attn_w_o, ffn_w_in, ffn_w_out, norm_final):
    wts = dict(norm_mix=norm_mix, norm_ffn=norm_ffn, conv_b_in=conv_b_in, conv_w_dw=conv_w_dw,
               conv_b_dw=conv_b_dw, conv_ln_g=conv_ln_g, conv_ln_b=conv_ln_b, conv_b_out=conv_b_out,
               norm_kv=norm_kv, norm_final=norm_final)
    for name, w in (("conv_w_in", conv_w_in), ("conv_w_out", conv_w_out), ("w_kv", w_kv[None]),
                    ("attn_w_q", attn_w_q), ("attn_w_o", attn_w_o), ("ffn_w_in", ffn_w_in),
                    ("ffn_w_out", ffn_w_out)):
        wts[name] = w.astype(BF16)

    bp, sp, d = x_prompt.shape
    pos_p = jnp.arange(sp, dtype=jnp.int32)
    y_p, conv_p, k_p, v_p = _trunk(x_prompt.reshape(bp * sp, d), pos_p, bp, sp, None, None, wts)
    kv_out_p = [_kv_tail(k_p.reshape(bp * sp, QW), v_p.reshape(bp * sp, QW), g, bp, sp, min(w, sp))
                for g, (w, _) in enumerate(GROUPS)]

    bs, ss, _ = x_sample.shape
    caches = (cache_kv_g0, cache_kv_g1, cache_kv_g2)
    pos_s = jnp.tile(PAST_LEN + jnp.arange(ss, dtype=jnp.int32), bs)
    y_s, conv_s, k_s, v_s = _trunk(x_sample.reshape(bs * ss, d), pos_s, bs, ss, state_conv, caches, wts)
    kv_out_s = [_cache_shift(c, _kv_new_rows(k_s, v_s, g)) for g, c in enumerate(caches)]

    return (y_p.reshape(bp, sp, d), y_s.reshape(bs, ss, d), conv_p, *kv_out_p, conv_s, *kv_out_s)
```

```python
import functools

import jax
import jax.numpy as jnp
from jax import lax
from jax.experimental import pallas as pl
from jax.experimental.pallas import tpu as pltpu

F32 = jnp.float32
BF16 = jnp.bfloat16

D_MODEL = 2048
N_HEADS = 16
HEAD_DIM = 128
GROUPS = ((128, 1), (512, 4), (2048, 16))
N_GROUPS = len(GROUPS)
PAST_LEN = 16384
QBLK = 128
ATT_SPAN = QBLK * max(d for _, d in GROUPS)
ROT_DIM = HEAD_DIM // 4
ROPE_THETA = 500000.0
CONV_WIDTH = 31
CONV_STATE = CONV_WIDTH - 1
CONV_HALO = 32
CONV_TAPS = 40
RMS_EPS = 1e-6
LN_EPS = 1e-5
NEG = -1e30
SM_SCALE = HEAD_DIM ** -0.5
QW = N_GROUPS * N_HEADS * HEAD_DIM

V7X_VMEM_LIMIT = 56 * 1024 * 1024


def _params(n_axes):
    return pltpu.CompilerParams(dimension_semantics=("arbitrary",) * n_axes,
                                vmem_limit_bytes=V7X_VMEM_LIMIT)


def _rms(x, g):
    return x * lax.rsqrt(jnp.mean(x * x, axis=-1, keepdims=True) + RMS_EPS) * g


def _glu_kernel(x_ref, gn_ref, wa_ref, wg_ref, ba_ref, bg_ref, o_ref, xn_ref):
    @pl.when(pl.program_id(1) == 0)
    def _():
        xn_ref[...] = _rms(x_ref[...], gn_ref[...]).astype(BF16)

    xn = xn_ref[...]
    a = jnp.dot(xn, wa_ref[...], preferred_element_type=F32) + ba_ref[...]
    gate = jnp.dot(xn, wg_ref[...], preferred_element_type=F32) + bg_ref[...]
    o_ref[...] = a * jax.nn.sigmoid(gate)


def _glu(x, gn, w_in, b_in, layer, tm, tn=1024):
    m, d = x.shape
    nj = d // tn
    return pl.pallas_call(
        _glu_kernel,
        out_shape=jax.ShapeDtypeStruct((m, d), F32),
        grid=(m // tm, nj),
        in_specs=[
            pl.BlockSpec((tm, d), lambda i, j: (i, 0)),
            pl.BlockSpec((1, d), lambda i, j: (0, 0)),
            pl.BlockSpec((None, d, tn), lambda i, j: (layer, 0, j)),
            pl.BlockSpec((None, d, tn), lambda i, j: (layer, 0, j + nj)),
            pl.BlockSpec((1, tn), lambda i, j: (0, j)),
            pl.BlockSpec((1, tn), lambda i, j: (0, j + nj)),
        ],
        out_specs=pl.BlockSpec((tm, tn), lambda i, j: (i, j)),
        scratch_shapes=[pltpu.VMEM((tm, d), BF16)],
        compiler_params=_params(2),
        name="glu_proj",
    )(x, gn, w_in, w_in, b_in, b_in)


def _conv_taps(buf_ref, w_ref, r0, c0, rows):
    acc = None
    for s in range(8):
        win = rows if s == 0 else rows + 8
        u = None
        for a in range(CONV_TAPS // 8):
            k = 8 * a + s
            if k < CONV_HALO - CONV_STATE or k > CONV_HALO:
                continue
            term = buf_ref[pl.ds(r0 + 8 * a, win), pl.ds(c0, 128)] * w_ref[k:k + 1, pl.ds(c0, 128)]
            u = term if u is None else u + term
        part = u if s == 0 else u[s:s + rows]
        acc = part if acc is None else acc + part
    return acc


def _ln_silu(y, g, b):
    mu = jnp.mean(y, axis=-1, keepdims=True)
    dlt = y - mu
    var = jnp.mean(dlt * dlt, axis=-1, keepdims=True)
    z = dlt * lax.rsqrt(var + LN_EPS) * g + b
    return z * jax.nn.sigmoid(z)


def _conv_prompt_kernel(cur_ref, halo_ref, w_ref, bdw_ref, lg_ref, lb_ref, o_ref,
                        buf_ref, y_ref, *, tiles_per_seq, tm, rc):
    i = pl.program_id(0)
    buf_ref[0:CONV_HALO, :] = jnp.where(i % tiles_per_seq == 0, 0.0, halo_ref[...])
    buf_ref[CONV_HALO:, :] = cur_ref[...]

    def col_body(c, carry):
        c0 = pl.multiple_of(c * 128, 128)

        def row_body(r, carry2):
            r0 = pl.multiple_of(r * rc, rc)
            y_ref[pl.ds(r0, rc), pl.ds(c0, 128)] = _conv_taps(buf_ref, w_ref, r0, c0, rc)
            return carry2

        return lax.fori_loop(0, tm // rc, row_body, carry)

    lax.fori_loop(0, D_MODEL // 128, col_body, 0)
    y = y_ref[...] + bdw_ref[...]
    o_ref[...] = _ln_silu(y, lg_ref[...], lb_ref[...]).astype(o_ref.dtype)


def _conv_prompt(g, w_taps, b_dw, ln_g, ln_b, seq, tm=256, rc=64):
    m, d = g.shape
    hb = tm // CONV_HALO
    kern = functools.partial(_conv_prompt_kernel, tiles_per_seq=seq // tm, tm=tm, rc=rc)
    return pl.pallas_call(
        kern,
        out_shape=jax.ShapeDtypeStruct((m, d), BF16),
        grid=(m // tm,),
        in_specs=[
            pl.BlockSpec((tm, d), lambda i: (i, 0)),
            pl.BlockSpec((CONV_HALO, d), lambda i: (jnp.maximum(i * hb - 1, 0), 0)),
            pl.BlockSpec((CONV_TAPS, d), lambda i: (0, 0)),
            pl.BlockSpec((1, d), lambda i: (0, 0)),
            pl.BlockSpec((1, d), lambda i: (0, 0)),
            pl.BlockSpec((1, d), lambda i: (0, 0)),
        ],
        out_specs=pl.BlockSpec((tm, d), lambda i: (i, 0)),
        scratch_shapes=[pltpu.VMEM((tm + CONV_HALO, d), F32), pltpu.VMEM((tm, d), F32)],
        compiler_params=_params(1),
        name="conv_prompt",
    )(g, g, w_taps, b_dw, ln_g, ln_b)


def _conv_sample_kernel(buf_ref, w_ref, bdw_ref, lg_ref, lb_ref, o_ref, *, t_new):
    acc = None
    for k in range(CONV_HALO - CONV_STATE, CONV_HALO + 1):
        term = buf_ref[k:k + t_new, :] * w_ref[k:k + 1, :]
        acc = term if acc is None else acc + term
    y = acc + bdw_ref[...]
    o_ref[...] = _ln_silu(y, lg_ref[...], lb_ref[...])


def _conv_sample(buf, w_taps, b_dw, ln_g, ln_b, t_new):
    nb, rows, d = buf.shape
    kern = functools.partial(_conv_sample_kernel, t_new=t_new)
    return pl.pallas_call(
        kern,
        out_shape=jax.ShapeDtypeStruct((nb, t_new, d), F32),
        grid=(nb,),
        in_specs=[
            pl.BlockSpec((None, rows, d), lambda b: (b, 0, 0)),
            pl.BlockSpec((CONV_TAPS, d), lambda b: (0, 0)),
            pl.BlockSpec((1, d), lambda b: (0, 0)),
            pl.BlockSpec((1, d), lambda b: (0, 0)),
            pl.BlockSpec((1, d), lambda b: (0, 0)),
        ],
        out_specs=pl.BlockSpec((None, t_new, d), lambda b: (b, 0, 0)),
        compiler_params=_params(1),
        name="conv_sample",
    )(buf, w_taps, b_dw, ln_g, ln_b)


def _out_proj_kernel(y_ref, w_ref, b_ref, x_ref, o_ref):
    acc = jnp.dot(y_ref[...].astype(BF16), w_ref[...], preferred_element_type=F32)
    o_ref[...] = x_ref[...] + acc + b_ref[...]


def _out_proj(y, w, b, x, layer, tm):
    m, d = x.shape
    k = y.shape[1]
    return pl.pallas_call(
        _out_proj_kernel,
        out_shape=jax.ShapeDtypeStruct((m, d), F32),
        grid=(m // tm,),
        in_specs=[
            pl.BlockSpec((tm, k), lambda i: (i, 0)),
            pl.BlockSpec((None, k, d), lambda i: (layer, 0, 0)),
            pl.BlockSpec((1, d), lambda i: (0, 0)),
            pl.BlockSpec((tm, d), lambda i: (i, 0)),
        ],
        out_specs=pl.BlockSpec((tm, d), lambda i: (i, 0)),
        compiler_params=_params(1),
        name="out_proj",
    )(y, w, b, x)


def _ffn_kernel(x_ref, gn_ref, wg_ref, wu_ref, wo_ref, gf_ref, o_ref, xn_ref, *, final_norm):
    f = pl.program_id(1)

    @pl.when(f == 0)
    def _():
        x = x_ref[...]
        xn_ref[...] = _rms(x, gn_ref[...]).astype(BF16)
        o_ref[...] = x

    xn = xn_ref[...]
    gt = jnp.dot(xn, wg_ref[...], preferred_element_type=F32)
    up = jnp.dot(xn, wu_ref[...], preferred_element_type=F32)
    h = (gt * jax.nn.sigmoid(gt) * up).astype(BF16)
    o_ref[...] += jnp.dot(h, wo_ref[...], preferred_element_type=F32)

    if final_norm:
        @pl.when(f == pl.num_programs(1) - 1)
        def _():
            o_ref[...] = _rms(o_ref[...], gf_ref[...])


def _ffn(x, gn, w_in, w_out, gf, layer, tm, final_norm, tf=512):
    m, d = x.shape
    dff = w_out.shape[1]
    nf = dff // tf
    kern = functools.partial(_ffn_kernel, final_norm=final_norm)
    return pl.pallas_call(
        kern,
        out_shape=jax.ShapeDtypeStruct((m, d), F32),
        grid=(m // tm, nf),
        in_specs=[
            pl.BlockSpec((tm, d), lambda i, f: (i, 0)),
            pl.BlockSpec((1, d), lambda i, f: (0, 0)),
            pl.BlockSpec((None, d, tf), lambda i, f: (layer, 0, f)),
            pl.BlockSpec((None, d, tf), lambda i, f: (layer, 0, f + nf)),
            pl.BlockSpec((None, tf, d), lambda i, f: (layer, f, 0)),
            pl.BlockSpec((1, d), lambda i, f: (0, 0)),
        ],
        out_specs=pl.BlockSpec((tm, d), lambda i, f: (i, 0)),
        scratch_shapes=[pltpu.VMEM((tm, d), BF16)],
        compiler_params=_params(2),
        name="ffn",
    )(x, gn, w_in, w_in, w_out, gf)


def _rope(acc, cos_t, sin_a, sin_b):
    outs = []
    for hh in range(acc.shape[1] // HEAD_DIM):
        xs = acc[:, hh * HEAD_DIM:(hh + 1) * HEAD_DIM]
        outs.append(xs * cos_t
                    + pltpu.roll(xs, HEAD_DIM - ROT_DIM // 2, axis=1) * sin_a
                    + pltpu.roll(xs, ROT_DIM // 2, axis=1) * sin_b)
    return jnp.concatenate(outs, axis=1)


def _proj_kernel(x_ref, gn_ref, w_ref, o_ref, xn_ref):
    @pl.when(pl.program_id(1) == 0)
    def _():
        xn_ref[...] = _rms(x_ref[...], gn_ref[...]).astype(BF16)

    o_ref[...] = jnp.dot(xn_ref[...], w_ref[...], preferred_element_type=F32)


def _proj(x, gn, w, layer, col0, n, tm, tn=1024):
    m, d = x.shape
    cb0 = col0 // tn
    return pl.pallas_call(
        _proj_kernel,
        out_shape=jax.ShapeDtypeStruct((m, n), F32),
        grid=(m // tm, n // tn),
        in_specs=[
            pl.BlockSpec((tm, d), lambda i, j: (i, 0)),
            pl.BlockSpec((1, d), lambda i, j: (0, 0)),
            pl.BlockSpec((None, d, tn), lambda i, j: (layer, 0, cb0 + j)),
        ],
        out_specs=pl.BlockSpec((tm, tn), lambda i, j: (i, j)),
        scratch_shapes=[pltpu.VMEM((tm, d), BF16)],
        compiler_params=_params(2),
        name="proj",
    )(x, gn, w)


def _proj_rope_kernel(x_ref, gn_ref, w_ref, cos_ref, sa_ref, sb_ref, o_ref, xn_ref, raw_ref, *, nj):
    t = pl.program_id(0)

    @pl.when(t % nj == 0)
    def _():
        xn_ref[...] = _rms(x_ref[...], gn_ref[...]).astype(BF16)

    @pl.when(t == 0)
    def _():
        raw_ref[...] = jnp.zeros_like(raw_ref)

    o_ref[...] = _rope(raw_ref[...], cos_ref[...], sa_ref[...], sb_ref[...])
    raw_ref[...] = jnp.dot(xn_ref[...], w_ref[...], preferred_element_type=F32)


def _proj_rope(x, gn, w, layer, col0, n, tabs, tm, tn=1024):
    m, d = x.shape
    cos_t, sin_a, sin_b = tabs
    tab_blocks = cos_t.shape[0] // tm
    cb0 = col0 // tn
    nj = n // tn
    tiles = (m // tm) * nj
    kern = functools.partial(_proj_rope_kernel, nj=nj)
    done = lambda t: jnp.maximum(t - 1, 0)
    tab_spec = pl.BlockSpec((tm, HEAD_DIM), lambda t: ((done(t) // nj) % tab_blocks, 0))
    return pl.pallas_call(
        kern,
        out_shape=jax.ShapeDtypeStruct((m, n), F32),
        grid=(tiles + 1,),
        in_specs=[
            pl.BlockSpec((tm, d), lambda t: (jnp.minimum(t, tiles - 1) // nj, 0)),
            pl.BlockSpec((1, d), lambda t: (0, 0)),
            pl.BlockSpec((None, d, tn), lambda t: (layer, 0, cb0 + t % nj)),
            tab_spec, tab_spec, tab_spec,
        ],
        out_specs=pl.BlockSpec((tm, tn), lambda t: (done(t) // nj, done(t) % nj)),
        scratch_shapes=[pltpu.VMEM((tm, d), BF16), pltpu.VMEM((tm, tn), F32)],
        compiler_params=_params(1),
        name="proj_rope",
    )(x, gn, w, cos_t, sin_a, sin_b)


def _rope_tables(pos):
    half = ROT_DIM // 2
    inv = ROPE_THETA ** (-jnp.arange(half, dtype=F32) / half)
    ang = pos.astype(F32)[:, None] * inv[None, :]
    cos, sin = jnp.cos(ang), jnp.sin(ang)
    n = pos.shape[0]
    ones = jnp.ones((n, HEAD_DIM - ROT_DIM), F32)
    zeros = jnp.zeros((n, HEAD_DIM - ROT_DIM), F32)
    zh = jnp.zeros((n, half), F32)
    cos_t = jnp.concatenate([cos, cos, ones], axis=1)
    sin_a = jnp.concatenate([-sin, zh, zeros], axis=1)
    sin_b = jnp.concatenate([zh, sin, zeros], axis=1)
    return cos_t, sin_a, sin_b


def _attn_prompt_kernel(q0, q1, q2, k0, k1, k2, v0, v1, v2, o_ref,
                        s_scr, m_scr, num_scr, den_scr, mx_scr, kcar_scr, vcar_scr):
    has_prev = pl.program_id(2) > 0

    @pl.when(jnp.logical_not(has_prev))
    def _():
        kcar_scr[...] = jnp.zeros_like(kcar_scr)
        vcar_scr[...] = jnp.zeros_like(vcar_scr)

    qi = lax.broadcasted_iota(jnp.int32, (QBLK, 2 * QBLK), 0)
    ci = lax.broadcasted_iota(jnp.int32, (QBLK, 2 * QBLK), 1)
    mask = jnp.logical_and(ci >= qi, ci <= qi + QBLK)
    mask0 = jnp.logical_and(mask, jnp.logical_or(ci >= QBLK, has_prev))
    ones = jnp.ones((2 * QBLK, HEAD_DIM), BF16)
    nt = (((1,), (1,)), ((), ()))
    refs = ((q0, k0, v0), (q1, k1, v1), (q2, k2, v2))
    car0 = 0

    for g, (_, dil) in enumerate(GROUPS):
        q_ref, kc_ref, vc_ref = refs[g]
        nblk = ATT_SPAN // (QBLK * dil)
        units = [(jb, r) for jb in range(nblk) for r in range(dil)]

        def rows(ref, jb, r, dil=dil):
            start = QBLK * jb * dil + r
            if dil == 1:
                return ref[pl.ds(start, QBLK), :]
            return ref[pl.ds(start, QBLK, stride=dil), :]

        def put(scr, jb, r, val, g=g, dil=dil):
            start = QBLK * jb * dil + r
            if dil == 1:
                scr[g, pl.ds(start, QBLK), :] = val
            else:
                scr[g, pl.ds(start, QBLK, stride=dil), :] = val

        for u, (jb, r) in enumerate(units):
            q = rows(q_ref, jb, r).astype(BF16)
            kc = rows(kc_ref, jb, r).astype(BF16)
            if jb > 0:
                kp, mk = rows(kc_ref, jb - 1, r).astype(BF16), mask
            else:
                kp, mk = kcar_scr[car0 + r], mask0
            if jb == nblk - 1:
                kcar_scr[car0 + r] = kc
            s = lax.dot_general(q, jnp.concatenate([kp, kc], axis=0), nt,
                                preferred_element_type=F32) * SM_SCALE
            s = jnp.where(mk, s, NEG)
            s_scr[u] = s
            mx = jnp.max(jnp.maximum(s[:, 0:QBLK], s[:, QBLK:]), axis=-1, keepdims=True)
            m_scr[u] = jnp.broadcast_to(mx, (QBLK, HEAD_DIM))

        for u, (jb, r) in enumerate(units):
            vc = rows(vc_ref, jb, r).astype(BF16)
            vp = rows(vc_ref, jb - 1, r).astype(BF16) if jb > 0 else vcar_scr[car0 + r]
            if jb == nblk - 1:
                vcar_scr[car0 + r] = vc
            mx = m_scr[u]
            p = jnp.concatenate([jnp.exp(s_scr[u, :, 0:QBLK] - mx), jnp.exp(s_scr[u, :, QBLK:] - mx)],
                                axis=1).astype(BF16)
            vals = jnp.concatenate([jnp.concatenate([vp, vc], axis=0), ones], axis=1)
            acc = jnp.dot(p, vals, preferred_element_type=F32)
            put(num_scr, jb, r, acc[:, 0:HEAD_DIM])
            put(den_scr, jb, r, acc[:, HEAD_DIM:])
            put(mx_scr, jb, r, m_scr[u])
        car0 += dil

    ch = 64
    for c in range(ATT_SPAN // ch):
        rs = pl.ds(c * ch, ch)
        m0, m1, m2 = mx_scr[0, rs, :], mx_scr[1, rs, :], mx_scr[2, rs, :]
        mm = jnp.maximum(jnp.maximum(m0, m1), m2)
        a0, a1, a2 = jnp.exp(m0 - mm), jnp.exp(m1 - mm), jnp.exp(m2 - mm)
        num = a0 * num_scr[0, rs, :] + a1 * num_scr[1, rs, :] + a2 * num_scr[2, rs, :]
        den = a0 * den_scr[0, rs, :] + a1 * den_scr[1, rs, :] + a2 * den_scr[2, rs, :]
        o_ref[rs, :] = (num / den).astype(o_ref.dtype)


def _attn_prompt(q, k, v, batch, seq):
    m = q.shape[0]
    assert seq % ATT_SPAN == 0
    nspan = seq // ATT_SPAN

    def cur(g):
        return pl.BlockSpec((ATT_SPAN, HEAD_DIM), lambda b, h, s: (b * nspan + s, g * N_HEADS + h))

    gs = range(N_GROUPS)
    n_carry = sum(d for _, d in GROUPS)
    acc = lambda: pltpu.VMEM((N_GROUPS, ATT_SPAN, HEAD_DIM), F32)
    car = lambda: pltpu.VMEM((n_carry, QBLK, HEAD_DIM), BF16)
    return pl.pallas_call(
        _attn_prompt_kernel,
        out_shape=jax.ShapeDtypeStruct((m, N_HEADS * HEAD_DIM), BF16),
        grid=(batch, N_HEADS, nspan),
        in_specs=[cur(g) for g in gs] * 3,
        out_specs=pl.BlockSpec((ATT_SPAN, HEAD_DIM), lambda b, h, s: (b * nspan + s, h)),
        scratch_shapes=[
            pltpu.VMEM((N_HEADS, QBLK, 2 * QBLK), F32),
            pltpu.VMEM((N_HEADS, QBLK, HEAD_DIM), F32),
            acc(), acc(), acc(), car(), car(),
        ],
        compiler_params=_params(3),
        name="attn_prompt",
    )(q, q, q, k, k, k, v, v, v)


def _attn_sample_kernel(q_ref, kn_ref, vn_ref, c0_ref, c1_ref, c2_ref, o_ref, *, t_new):
    t = pl.program_id(1)
    caches = (c0_ref, c1_ref, c2_ref)
    outs, lses = [], []
    for g, (win, dil) in enumerate(GROUPS):
        c_ref = caches[g]
        n_c = win // dil
        res = t & (dil - 1)
        qg = q_ref[g]
        kc, vc = c_ref[:, 0], c_ref[:, 1]
        kn, vn = kn_ref[:, g], vn_ref[:, g]
        sc = jnp.sum(kc * qg[None], axis=-1, keepdims=True) * SM_SCALE
        sn = jnp.sum(kn * qg[None], axis=-1, keepdims=True) * SM_SCALE
        mi = lax.broadcasted_iota(jnp.int32, (n_c, N_HEADS, 1), 0)
        sc = jnp.where(mi * dil + res >= t, sc, NEG)
        ti = lax.broadcasted_iota(jnp.int32, (t_new, N_HEADS, 1), 0)
        vis = jnp.logical_and(ti <= t, ((t - ti) & (dil - 1)) == 0)
        sn = jnp.where(vis, sn, NEG)
        mx = jnp.maximum(jnp.max(sc, axis=0, keepdims=True), jnp.max(sn, axis=0, keepdims=True))
        pc = jnp.exp(sc - mx)
        pn = jnp.exp(sn - mx)
        den = jnp.sum(pc, axis=0, keepdims=True) + jnp.sum(pn, axis=0, keepdims=True)
        o = (jnp.sum(pc * vc, axis=0, keepdims=True) + jnp.sum(pn * vn, axis=0, keepdims=True)) / den
        outs.append(o[0])
        lses.append((mx + jnp.log(den))[0])
    lmx = jnp.maximum(jnp.maximum(lses[0], lses[1]), lses[2])
    es = [jnp.exp(l - lmx) for l in lses]
    tot = es[0] + es[1] + es[2]
    o_ref[...] = (outs[0] * (es[0] / tot) + outs[1] * (es[1] / tot) + outs[2] * (es[2] / tot))


def _attn_sample(q, k_new, v_new, caches, nbatch, t_new):
    shp = (nbatch, t_new, N_GROUPS, N_HEADS, HEAD_DIM)
    cviews, cspecs = [], []
    for (win, dil), c in zip(GROUPS, caches):
        n_c = win // dil
        cviews.append(c.reshape(nbatch, n_c, dil, 2, N_HEADS, HEAD_DIM))
        cspecs.append(pl.BlockSpec((None, n_c, None, 2, N_HEADS, HEAD_DIM),
                                   lambda b, t, dil=dil: (b, 0, lax.rem(t, dil), 0, 0, 0)))
    new_spec = pl.BlockSpec((None, t_new, N_GROUPS, N_HEADS, HEAD_DIM), lambda b, t: (b, 0, 0, 0, 0))
    kern = functools.partial(_attn_sample_kernel, t_new=t_new)
    o = pl.pallas_call(
        kern,
        out_shape=jax.ShapeDtypeStruct((nbatch, t_new, N_HEADS, HEAD_DIM), F32),
        grid=(nbatch, t_new),
        in_specs=[
            pl.BlockSpec((None, None, N_GROUPS, N_HEADS, HEAD_DIM), lambda b, t: (b, t, 0, 0, 0)),
            new_spec, new_spec, *cspecs,
        ],
        out_specs=pl.BlockSpec((None, None, N_HEADS, HEAD_DIM), lambda b, t: (b, t, 0, 0)),
        compiler_params=_params(2),
        name="attn_sample",
    )(q.reshape(shp), k_new.reshape(shp), v_new.reshape(shp), *cviews)
    return o.reshape(nbatch * t_new, N_HEADS * HEAD_DIM)


def _kv_tail_kernel(k_ref, v_ref, o_ref, *, rows):
    for kv, ref in enumerate((k_ref, v_ref)):
        for h in range(N_HEADS):
            o_ref[pl.ds(kv * N_HEADS + h, rows, stride=2 * N_HEADS), :] = (
                ref[:, h * HEAD_DIM:(h + 1) * HEAD_DIM])


def _kv_tail(k, v, g, batch, seq, win, tr=128):
    gw = N_HEADS * HEAD_DIM
    nblk = win // tr
    first = (seq - win) // tr
    per_seq = seq // tr
    in_spec = pl.BlockSpec((tr, gw), lambda b, i: (b * per_seq + first + i, g))
    flat = pl.pallas_call(
        functools.partial(_kv_tail_kernel, rows=tr),
        out_shape=jax.ShapeDtypeStruct((batch * win * 2 * N_HEADS, HEAD_DIM), F32),
        grid=(batch, nblk),
        in_specs=[in_spec, in_spec],
        out_specs=pl.BlockSpec((tr * 2 * N_HEADS, HEAD_DIM), lambda b, i: (b * nblk + i, 0)),
        compiler_params=_params(2),
        name="kv_tail",
    )(k, v)
    return flat.reshape(batch, win, 2, N_HEADS, HEAD_DIM)


def _cache_shift_kernel(a_ref, nxt_ref, new_ref, o_ref, *, rows, t_new):
    o_ref[0:rows - t_new] = a_ref[t_new:rows]
    last = pl.program_id(1) == pl.num_programs(1) - 1
    o_ref[rows - t_new:rows] = jnp.where(last, new_ref[...], nxt_ref[...])


def _cache_shift(cache, new, tr=256):
    nbatch, win = cache.shape[:2]
    t_new = new.shape[1]
    tr = min(tr, win)
    nblk = win // tr
    tail = cache.shape[2:]
    zeros = (0,) * len(tail)
    return pl.pallas_call(
        functools.partial(_cache_shift_kernel, rows=tr, t_new=t_new),
        out_shape=jax.ShapeDtypeStruct(cache.shape, cache.dtype),
        grid=(nbatch, nblk),
        in_specs=[
            pl.BlockSpec((None, tr) + tail, lambda b, i: (b, i) + zeros),
            pl.BlockSpec((None, t_new) + tail,
                         lambda b, i: (b, jnp.minimum(i + 1, nblk - 1) * (tr // t_new)) + zeros),
            pl.BlockSpec((None, t_new) + tail, lambda b, i: (b, 0) + zeros),
        ],
        out_specs=pl.BlockSpec((None, tr) + tail, lambda b, i: (b, i) + zeros),
        compiler_params=_params(2),
        name="cache_shift",
    )(cache, cache, new)


def _trunk(x, pos, batch, seq, conv_state, kv_caches, wts):
    m = x.shape[0]
    tm = min(512, m)
    tm_wide = min(1024, m)
    tabs = _rope_tables(pos)
    n_a = wts["conv_w_in"].shape[0]
    depth = wts["ffn_w_in"].shape[0]
    row = lambda v: v.reshape(1, -1)
    zero_bias = jnp.zeros((1, D_MODEL), F32)
    new_conv = []
    k = v = None
    for l in range(depth):
        if l < n_a:
            w_taps = jnp.pad(wts["conv_w_dw"][l],
                             ((CONV_HALO - CONV_STATE, CONV_TAPS - CONV_HALO - 1), (0, 0)))
            cargs = (w_taps, row(wts["conv_b_dw"][l]), row(wts["conv_ln_g"][l]), row(wts["conv_ln_b"][l]))
            g = _glu(x, row(wts["norm_mix"][l]), wts["conv_w_in"], row(wts["conv_b_in"][l]), l, tm_wide)
            g3 = g.reshape(batch, seq, D_MODEL)
            if conv_state is None:
                y = _conv_prompt(g, *cargs, seq=seq)
                new_conv.append(g3[:, seq - CONV_STATE:])
            else:
                buf = jnp.concatenate([conv_state[l], g3], axis=1)
                new_conv.append(buf[:, -CONV_STATE:])
                lead = CONV_HALO - CONV_STATE
                buf = jnp.pad(buf, ((0, 0), (lead, CONV_TAPS - lead - CONV_STATE - seq), (0, 0)))
                y = _conv_sample(buf, *cargs, t_new=seq).reshape(m, D_MODEL)
            x = _out_proj(y, wts["conv_w_out"], row(wts["conv_b_out"][l]), x, l, tm)
        else:
            j = l - n_a
            if j == 0:
                gkv = row(wts["norm_kv"])
                k = _proj_rope(x, gkv, wts["w_kv"], 0, 0, QW, tabs, tm_wide)
                v = _proj(x, gkv, wts["w_kv"], 0, QW, QW, tm_wide)
            q = _proj_rope(x, row(wts["norm_mix"][l]), wts["attn_w_q"], j, 0, QW, tabs, tm_wide)
            if kv_caches is None:
                o = _attn_prompt(q, k, v, batch, seq)
            else:
                o = _attn_sample(q, k, v, kv_caches, batch, seq)
            x = _out_proj(o, wts["attn_w_o"], zero_bias, x, j, tm)
        x = _ffn(x, row(wts["norm_ffn"][l]), wts["ffn_w_in"], wts["ffn_w_out"],
                 row(wts["norm_final"]), l, tm_wide, final_norm=(l == depth - 1))
    return x, jnp.stack(new_conv, axis=0), k.reshape(batch, seq, QW), v.reshape(batch, seq, QW)


def _kv_new_rows(k3, v3, g):
    gw = N_HEADS * HEAD_DIM
    shp = k3.shape[:2] + (N_HEADS, HEAD_DIM)
    return jnp.stack([a[:, :, g * gw:(g + 1) * gw].reshape(shp) for a in (k3, v3)], axis=2)


def kernel(x_prompt, x_sample, state_conv, cache_kv_g0, cache_kv_g1, cache_kv_g2, norm_mix, norm_ffn, conv_w_in, conv_b_in, conv_w_dw, conv_b_dw, conv_ln_g, conv_ln_b, conv_w_out, conv_b_out, norm_kv, w_kv, attn_w_q, attn_w_o, ffn_w_in, ffn_w_out, norm_final):
    wts = dict(norm_mix=norm_mix, norm_ffn=norm_ffn, conv_b_in=conv_b_in, conv_w_dw=conv_w_dw,
               conv_b_dw=conv_b_dw, conv_ln_g=conv_ln_g, conv_ln_b=conv_ln_b, conv_b_out=conv_b_out,
               norm_kv=norm_kv, norm_final=norm_final)
    for name, w in (("conv_w_in", conv_w_in), ("conv_w_out", conv_w_out), ("w_kv", w_kv[None]),
                    ("attn_w_q", attn_w_q), ("attn_w_o", attn_w_o), ("ffn_w_in", ffn_w_in),
                    ("ffn_w_out", ffn_w_out)):
        wts[name] = w.astype(BF16)

    bp, sp, d = x_prompt.shape
    pos_p = jnp.arange(sp, dtype=jnp.int32)
    y_p, conv_p, k_p, v_p = _trunk(x_prompt.reshape(bp * sp, d), pos_p, bp, sp, None, None, wts)
    kv_out_p = [_kv_tail(k_p.reshape(bp * sp, QW), v_p.reshape(bp * sp, QW), g, bp, sp, min(w, sp))
                for g, (w, _) in enumerate(GROUPS)]

    bs, ss, _ = x_sample.shape
    caches = (cache_kv_g0, cache_kv_g1, cache_kv_g2)
    pos_s = jnp.tile(PAST_LEN + jnp.arange(ss, dtype=jnp.int32), bs)
    y_s, conv_s, k_s, v_s = _trunk(x_sample.reshape(bs * ss, d), pos_s, bs, ss, state_conv, caches, wts)
    kv_out_s = [_cache_shift(c, _kv_new_rows(k_s, v_s, g)) for g, c in enumerate(caches)]

    return (y_p.reshape(bp, sp, d), y_s.reshape(bs, ss, d), conv_p, *kv_out_p, conv_s, *kv_out_s)
```

```python
import functools

import jax
import jax.numpy as jnp
from jax import lax
from jax.experimental import pallas as pl
from jax.experimental.pallas import tpu as pltpu

F32 = jnp.float32
BF16 = jnp.bfloat16

D_MODEL = 2048
N_HEADS = 16
HEAD_DIM = 128
GROUPS = ((128, 1), (512, 4), (2048, 16))
N_GROUPS = len(GROUPS)
PAST_LEN = 16384
QBLK = 128
ATT_SPAN = QBLK * max(d for _, d in GROUPS)
ROT_DIM = HEAD_DIM // 4
ROPE_THETA = 500000.0
CONV_WIDTH = 31
CONV_STATE = CONV_WIDTH - 1
CONV_HALO = 32
CONV_TAPS = 40
RMS_EPS = 1e-6
LN_EPS = 1e-5
NEG = -1e30
SM_SCALE = HEAD_DIM ** -0.5
QW = N_GROUPS * N_HEADS * HEAD_DIM

V7X_VMEM_LIMIT = 56 * 1024 * 1024


def _params(n_axes):
    return pltpu.CompilerParams(dimension_semantics=("arbitrary",) * n_axes,
                                vmem_limit_bytes=V7X_VMEM_LIMIT)


def _rms(x, g):
    return x * lax.rsqrt(jnp.mean(x * x, axis=-1, keepdims=True) + RMS_EPS) * g


def _glu_kernel(x_ref, gn_ref, wa_ref, wg_ref, ba_ref, bg_ref, o_ref, xn_ref):
    @pl.when(pl.program_id(1) == 0)
    def _():
        xn_ref[...] = _rms(x_ref[...], gn_ref[...]).astype(BF16)

    xn = xn_ref[...]
    a = jnp.dot(xn, wa_ref[...], preferred_element_type=F32) + ba_ref[...]
    gate = jnp.dot(xn, wg_ref[...], preferred_element_type=F32) + bg_ref[...]
    o_ref[...] = a * jax.nn.sigmoid(gate)


def _glu(x, gn, w_in, b_in, layer, tm, tn=1024):
    m, d = x.shape
    nj = d // tn
    return pl.pallas_call(
        _glu_kernel,
        out_shape=jax.ShapeDtypeStruct((m, d), F32),
        grid=(m // tm, nj),
        in_specs=[
            pl.BlockSpec((tm, d), lambda i, j: (i, 0)),
            pl.BlockSpec((1, d), lambda i, j: (0, 0)),
            pl.BlockSpec((None, d, tn), lambda i, j: (layer, 0, j)),
            pl.BlockSpec((None, d, tn), lambda i, j: (layer, 0, j + nj)),
            pl.BlockSpec((1, tn), lambda i, j: (0, j)),
            pl.BlockSpec((1, tn), lambda i, j: (0, j + nj)),
        ],
        out_specs=pl.BlockSpec((tm, tn), lambda i, j: (i, j)),
        scratch_shapes=[pltpu.VMEM((tm, d), BF16)],
        compiler_params=_params(2),
        name="glu_proj",
    )(x, gn, w_in, w_in, b_in, b_in)


def _conv_taps(buf_ref, w_ref, r0, c0, rows):
    acc = None
    for s in range(8):
        win = rows if s == 0 else rows + 8
        u = None
        for a in range(CONV_TAPS // 8):
            k = 8 * a + s
            if k < CONV_HALO - CONV_STATE or k > CONV_HALO:
                continue
            term = buf_ref[pl.ds(r0 + 8 * a, win), pl.ds(c0, 128)] * w_ref[k:k + 1, pl.ds(c0, 128)]
            u = term if u is None else u + term
        part = u if s == 0 else u[s:s + rows]
        acc = part if acc is None else acc + part
    return acc


def _ln_silu(y, g, b):
    mu = jnp.mean(y, axis=-1, keepdims=True)
    dlt = y - mu
    var = jnp.mean(dlt * dlt, axis=-1, keepdims=True)
    z = dlt * lax.rsqrt(var + LN_EPS) * g + b
    return z * jax.nn.sigmoid(z)


def _conv_prompt_kernel(cur_ref, halo_ref, w_ref, bdw_ref, lg_ref, lb_ref, o_ref,
                        buf_ref, y_ref, *, tiles_per_seq, tm, rc):
    i = pl.program_id(0)
    buf_ref[0:CONV_HALO, :] = jnp.where(i % tiles_per_seq == 0, 0.0, halo_ref[...])
    buf_ref[CONV_HALO:, :] = cur_ref[...]

    def col_body(c, carry):
        c0 = pl.multiple_of(c * 128, 128)

        def row_body(r, carry2):
            r0 = pl.multiple_of(r * rc, rc)
            y_ref[pl.ds(r0, rc), pl.ds(c0, 128)] = _conv_taps(buf_ref, w_ref, r0, c0, rc)
            return carry2

        return lax.fori_loop(0, tm // rc, row_body, carry)

    lax.fori_loop(0, D_MODEL // 128, col_body, 0)
    y = y_ref[...] + bdw_ref[...]
    o_ref[...] = _ln_silu(y, lg_ref[...], lb_ref[...]).astype(o_ref.dtype)


def _conv_prompt(g, w_taps, b_dw, ln_g, ln_b, seq, tm=512, rc=64):
    m, d = g.shape
    hb = tm // CONV_HALO
    kern = functools.partial(_conv_prompt_kernel, tiles_per_seq=seq // tm, tm=tm, rc=rc)
    return pl.pallas_call(
        kern,
        out_shape=jax.ShapeDtypeStruct((m, d), BF16),
        grid=(m // tm,),
        in_specs=[
            pl.BlockSpec((tm, d), lambda i: (i, 0)),
            pl.BlockSpec((CONV_HALO, d), lambda i: (jnp.maximum(i * hb - 1, 0), 0)),
            pl.BlockSpec((CONV_TAPS, d), lambda i: (0, 0)),
            pl.BlockSpec((1, d), lambda i: (0, 0)),
            pl.BlockSpec((1, d), lambda i: (0, 0)),
            pl.BlockSpec((1, d), lambda i: (0, 0)),
        ],
        out_specs=pl.BlockSpec((tm, d), lambda i: (i, 0)),
        scratch_shapes=[pltpu.VMEM((tm + CONV_HALO, d), F32), pltpu.VMEM((tm, d), F32)],
        compiler_params=_params(1),
        name="conv_prompt",
    )(g, g, w_taps, b_dw, ln_g, ln_b)


def _conv_sample_kernel(buf_ref, w_ref, bdw_ref, lg_ref, lb_ref, o_ref, *, t_new):
    acc = None
    for k in range(CONV_HALO - CONV_STATE, CONV_HALO + 1):
        term = buf_ref[k:k + t_new, :] * w_ref[k:k + 1, :]
        acc = term if acc is None else acc + term
    y = acc + bdw_ref[...]
    o_ref[...] = _ln_silu(y, lg_ref[...], lb_ref[...])


def _conv_sample(buf, w_taps, b_dw, ln_g, ln_b, t_new):
    nb, rows, d = buf.shape
    kern = functools.partial(_conv_sample_kernel, t_new=t_new)
    return pl.pallas_call(
        kern,
        out_shape=jax.ShapeDtypeStruct((nb, t_new, d), F32),
        grid=(nb,),
        in_specs=[
            pl.BlockSpec((None, rows, d), lambda b: (b, 0, 0)),
            pl.BlockSpec((CONV_TAPS, d), lambda b: (0, 0)),
            pl.BlockSpec((1, d), lambda b: (0, 0)),
            pl.BlockSpec((1, d), lambda b: (0, 0)),
            pl.BlockSpec((1, d), lambda b: (0, 0)),
        ],
        out_specs=pl.BlockSpec((None, t_new, d), lambda b: (b, 0, 0)),
        compiler_params=_params(1),
        name="conv_sample",
    )(buf, w_taps, b_dw, ln_g, ln_b)


def _out_proj_kernel(y_ref, w_ref, b_ref, x_ref, o_ref):
    acc = jnp.dot(y_ref[...].astype(BF16), w_ref[...], preferred_element_type=F32)
    o_ref[...] = x_ref[...] + acc + b_ref[...]


def _out_proj(y, w, b, x, layer, tm):
    m, d = x.shape
    k = y.shape[1]
    return pl.pallas_call(
        _out_proj_kernel,
        out_shape=jax.ShapeDtypeStruct((m, d), F32),
        grid=(m // tm,),
        in_specs=[
            pl.BlockSpec((tm, k), lambda i: (i, 0)),
            pl.BlockSpec((None, k, d), lambda i: (layer, 0, 0)),
            pl.BlockSpec((1, d), lambda i: (0, 0)),
            pl.BlockSpec((tm, d), lambda i: (i, 0)),
        ],
        out_specs=pl.BlockSpec((tm, d), lambda i: (i, 0)),
        compiler_params=_params(1),
        name="out_proj",
    )(y, w, b, x)


def _ffn_kernel(x_ref, gn_ref, wg_ref, wu_ref, wo_ref, gf_ref, o_ref, xn_ref, *, final_norm):
    f = pl.program_id(1)

    @pl.when(f == 0)
    def _():
        x = x_ref[...]
        xn_ref[...] = _rms(x, gn_ref[...]).astype(BF16)
        o_ref[...] = x

    xn = xn_ref[...]
    gt = jnp.dot(xn, wg_ref[...], preferred_element_type=F32)
    up = jnp.dot(xn, wu_ref[...], preferred_element_type=F32)
    h = (gt * jax.nn.sigmoid(gt) * up).astype(BF16)
    o_ref[...] += jnp.dot(h, wo_ref[...], preferred_element_type=F32)

    if final_norm:
        @pl.when(f == pl.num_programs(1) - 1)
        def _():
            o_ref[...] = _rms(o_ref[...], gf_ref[...])


def _ffn(x, gn, w_in, w_out, gf, layer, tm, final_norm, tf=512):
    m, d = x.shape
    dff = w_out.shape[1]
    nf = dff // tf
    kern = functools.partial(_ffn_kernel, final_norm=final_norm)
    return pl.pallas_call(
        kern,
        out_shape=jax.ShapeDtypeStruct((m, d), F32),
        grid=(m // tm, nf),
        in_specs=[
            pl.BlockSpec((tm, d), lambda i, f: (i, 0)),
            pl.BlockSpec((1, d), lambda i, f: (0, 0)),
            pl.BlockSpec((None, d, tf), lambda i, f: (layer, 0, f)),
            pl.BlockSpec((None, d, tf), lambda i, f: (layer, 0, f + nf)),
            pl.BlockSpec((None, tf, d), lambda i, f: (layer, f, 0)),
            pl.BlockSpec((1, d), lambda i, f: (0, 0)),
        ],
        out_specs=pl.BlockSpec((tm, d), lambda i, f: (i, 0)),
        scratch_shapes=[pltpu.VMEM((tm, d), BF16)],
        compiler_params=_params(2),
        name="ffn",
    )(x, gn, w_in, w_in, w_out, gf)


def _rope(acc, cos_t, sin_a, sin_b):
    outs = []
    for hh in range(acc.shape[1] // HEAD_DIM):
        xs = acc[:, hh * HEAD_DIM:(hh + 1) * HEAD_DIM]
        outs.append(xs * cos_t
                    + pltpu.roll(xs, HEAD_DIM - ROT_DIM // 2, axis=1) * sin_a
                    + pltpu.roll(xs, ROT_DIM // 2, axis=1) * sin_b)
    return jnp.concatenate(outs, axis=1)


def _proj_kernel(x_ref, gn_ref, w_ref, o_ref, xn_ref):
    @pl.when(pl.program_id(1) == 0)
    def _():
        xn_ref[...] = _rms(x_ref[...], gn_ref[...]).astype(BF16)

    o_ref[...] = jnp.dot(xn_ref[...], w_ref[...], preferred_element_type=F32)


def _proj(x, gn, w, layer, col0, n, tm, tn=1024):
    m, d = x.shape
    cb0 = col0 // tn
    return pl.pallas_call(
        _proj_kernel,
        out_shape=jax.ShapeDtypeStruct((m, n), F32),
        grid=(m // tm, n // tn),
        in_specs=[
            pl.BlockSpec((tm, d), lambda i, j: (i, 0)),
            pl.BlockSpec((1, d), lambda i, j: (0, 0)),
            pl.BlockSpec((None, d, tn), lambda i, j: (layer, 0, cb0 + j)),
        ],
        out_specs=pl.BlockSpec((tm, tn), lambda i, j: (i, j)),
        scratch_shapes=[pltpu.VMEM((tm, d), BF16)],
        compiler_params=_params(2),
        name="proj",
    )(x, gn, w)


def _proj_rope_kernel(x_ref, gn_ref, w_ref, cos_ref, sa_ref, sb_ref, o_ref, xn_ref, raw_ref, *, nj):
    t = pl.program_id(0)

    @pl.when(t % nj == 0)
    def _():
        xn_ref[...] = _rms(x_ref[...], gn_ref[...]).astype(BF16)

    @pl.when(t == 0)
    def _():
        raw_ref[...] = jnp.zeros_like(raw_ref)

    o_ref[...] = _rope(raw_ref[...], cos_ref[...], sa_ref[...], sb_ref[...])
    raw_ref[...] = jnp.dot(xn_ref[...], w_ref[...], preferred_element_type=F32)


def _proj_rope(x, gn, w, layer, col0, n, tabs, tm, tn=1024):
    m, d = x.shape
    cos_t, sin_a, sin_b = tabs
    tab_blocks = cos_t.shape[0] // tm
    cb0 = col0 // tn
    nj = n // tn
    tiles = (m // tm) * nj
    kern = functools.partial(_proj_rope_kernel, nj=nj)
    done = lambda t: jnp.maximum(t - 1, 0)
    tab_spec = pl.BlockSpec((tm, HEAD_DIM), lambda t: ((done(t) // nj) % tab_blocks, 0))
    return pl.pallas_call(
        kern,
        out_shape=jax.ShapeDtypeStruct((m, n), F32),
        grid=(tiles + 1,),
        in_specs=[
            pl.BlockSpec((tm, d), lambda t: (jnp.minimum(t, tiles - 1) // nj, 0)),
            pl.BlockSpec((1, d), lambda t: (0, 0)),
            pl.BlockSpec((None, d, tn), lambda t: (layer, 0, cb0 + t % nj)),
            tab_spec, tab_spec, tab_spec,
        ],
        out_specs=pl.BlockSpec((tm, tn), lambda t: (done(t) // nj, done(t) % nj)),
        scratch_shapes=[pltpu.VMEM((tm, d), BF16), pltpu.VMEM((tm, tn), F32)],
        compiler_params=_params(1),
        name="proj_rope",
    )(x, gn, w, cos_t, sin_a, sin_b)


def _rope_tables(pos):
    half = ROT_DIM // 2
    inv = ROPE_THETA ** (-jnp.arange(half, dtype=F32) / half)
    ang = pos.astype(F32)[:, None] * inv[None, :]
    cos, sin = jnp.cos(ang), jnp.sin(ang)
    n = pos.shape[0]
    ones = jnp.ones((n, HEAD_DIM - ROT_DIM), F32)
    zeros = jnp.zeros((n, HEAD_DIM - ROT_DIM), F32)
    zh = jnp.zeros((n, half), F32)
    cos_t = jnp.concatenate([cos, cos, ones], axis=1)
    sin_a = jnp.concatenate([-sin, zh, zeros], axis=1)
    sin_b = jnp.concatenate([zh, sin, zeros], axis=1)
    return cos_t, sin_a, sin_b


def _attn_prompt_kernel(q0, q1, q2, k0, k1, k2, v0, v1, v2, o_ref,
                        s_scr, m_scr, num_scr, den_scr, mx_scr, kcar_scr, vcar_scr):
    has_prev = pl.program_id(2) > 0

    @pl.when(jnp.logical_not(has_prev))
    def _():
        kcar_scr[...] = jnp.zeros_like(kcar_scr)
        vcar_scr[...] = jnp.zeros_like(vcar_scr)

    qi = lax.broadcasted_iota(jnp.int32, (QBLK, 2 * QBLK), 0)
    ci = lax.broadcasted_iota(jnp.int32, (QBLK, 2 * QBLK), 1)
    mask = jnp.logical_and(ci >= qi, ci <= qi + QBLK)
    mask0 = jnp.logical_and(mask, jnp.logical_or(ci >= QBLK, has_prev))
    ones = jnp.ones((2 * QBLK, HEAD_DIM), BF16)
    nt = (((1,), (1,)), ((), ()))
    refs = ((q0, k0, v0), (q1, k1, v1), (q2, k2, v2))
    car0 = 0

    for g, (_, dil) in enumerate(GROUPS):
        q_ref, kc_ref, vc_ref = refs[g]
        nblk = ATT_SPAN // (QBLK * dil)
        units = [(jb, r) for jb in range(nblk) for r in range(dil)]

        def rows(ref, jb, r, dil=dil):
            start = QBLK * jb * dil + r
            if dil == 1:
                return ref[pl.ds(start, QBLK), :]
            return ref[pl.ds(start, QBLK, stride=dil), :]

        def put(scr, jb, r, val, g=g, dil=dil):
            start = QBLK * jb * dil + r
            if dil == 1:
                scr[g, pl.ds(start, QBLK), :] = val
            else:
                scr[g, pl.ds(start, QBLK, stride=dil), :] = val

        for u, (jb, r) in enumerate(units):
            q = rows(q_ref, jb, r).astype(BF16)
            kc = rows(kc_ref, jb, r).astype(BF16)
            if jb > 0:
                kp, mk = rows(kc_ref, jb - 1, r).astype(BF16), mask
            else:
                kp, mk = kcar_scr[car0 + r], mask0
            if jb == nblk - 1:
                kcar_scr[car0 + r] = kc
            s = lax.dot_general(q, jnp.concatenate([kp, kc], axis=0), nt,
                                preferred_element_type=F32) * SM_SCALE
            s = jnp.where(mk, s, NEG)
            s_scr[u] = s
            mx = jnp.max(jnp.maximum(s[:, 0:QBLK], s[:, QBLK:]), axis=-1, keepdims=True)
            m_scr[u] = jnp.broadcast_to(mx, (QBLK, HEAD_DIM))

        for u, (jb, r) in enumerate(units):
            vc = rows(vc_ref, jb, r).astype(BF16)
            vp = rows(vc_ref, jb - 1, r).astype(BF16) if jb > 0 else vcar_scr[car0 + r]
            if jb == nblk - 1:
                vcar_scr[car0 + r] = vc
            mx = m_scr[u]
            p = jnp.concatenate([jnp.exp(s_scr[u, :, 0:QBLK] - mx), jnp.exp(s_scr[u, :, QBLK:] - mx)],
                                axis=1).astype(BF16)
            vals = jnp.concatenate([jnp.concatenate([vp, vc], axis=0), ones], axis=1)
            acc = jnp.dot(p, vals, preferred_element_type=F32)
            put(num_scr, jb, r, acc[:, 0:HEAD_DIM])
            put(den_scr, jb, r, acc[:, HEAD_DIM:])
            put(mx_scr, jb, r, mx)
        car0 += dil

    ch = 64
    for c in range(ATT_SPAN // ch):
        rs = pl.ds(c * ch, ch)
        m0, m1, m2 = mx_scr[0, rs, :], mx_scr[1, rs, :], mx_scr[2, rs, :]
        mm = jnp.maximum(jnp.maximum(m0, m1), m2)
        a0, a1, a2 = jnp.exp(m0 - mm), jnp.exp(m1 - mm), jnp.exp(m2 - mm)
        num = a0 * num_scr[0, rs, :] + a1 * num_scr[1, rs, :] + a2 * num_scr[2, rs, :]
        den = a0 * den_scr[0, rs, :] + a1 * den_scr[1, rs, :] + a2 * den_scr[2, rs, :]
        o_ref[rs, :] = (num / den).astype(o_ref.dtype)


def _attn_prompt(q, k, v, batch, seq):
    m = q.shape[0]
    assert seq % ATT_SPAN == 0
    nspan = seq // ATT_SPAN

    def cur(g):
        return pl.BlockSpec((ATT_SPAN, HEAD_DIM), lambda b, h, s: (b * nspan + s, g * N_HEADS + h))

    gs = range(N_GROUPS)
    n_carry = sum(d for _, d in GROUPS)
    acc = lambda: pltpu.VMEM((N_GROUPS, ATT_SPAN, HEAD_DIM), F32)
    car = lambda: pltpu.VMEM((n_carry, QBLK, HEAD_DIM), BF16)
    return pl.pallas_call(
        _attn_prompt_kernel,
        out_shape=jax.ShapeDtypeStruct((m, N_HEADS * HEAD_DIM), BF16),
        grid=(batch, N_HEADS, nspan),
        in_specs=[cur(g) for g in gs] * 3,
        out_specs=pl.BlockSpec((ATT_SPAN, HEAD_DIM), lambda b, h, s: (b * nspan + s, h)),
        scratch_shapes=[
            pltpu.VMEM((N_HEADS, QBLK, 2 * QBLK), F32),
            pltpu.VMEM((N_HEADS, QBLK, HEAD_DIM), F32),
            acc(), acc(), acc(), car(), car(),
        ],
        compiler_params=_params(3),
        name="attn_prompt",
    )(q, q, q, k, k, k, v, v, v)


def _attn_sample_kernel(q_ref, kn_ref, vn_ref, c0_ref, c1_ref, c2_ref, o_ref, *, t_new):
    caches = (c0_ref, c1_ref, c2_ref)
    for t in range(t_new):
        outs, lses = [], []
        for g, (win, dil) in enumerate(GROUPS):
            c_ref = caches[g]
            n_c = win // dil
            res = t % dil
            qg = q_ref[t, g]
            kc, vc = c_ref[:, res, 0], c_ref[:, res, 1]
            kn, vn = kn_ref[:, g], vn_ref[:, g]
            sc = jnp.sum(kc * qg[None], axis=-1, keepdims=True) * SM_SCALE
            sn = jnp.sum(kn * qg[None], axis=-1, keepdims=True) * SM_SCALE
            mi = lax.broadcasted_iota(jnp.int32, (n_c, N_HEADS, 1), 0)
            sc = jnp.where(mi * dil + res >= t, sc, NEG)
            ti = lax.broadcasted_iota(jnp.int32, (t_new, N_HEADS, 1), 0)
            vis = jnp.logical_and(ti <= t, ((t - ti) & (dil - 1)) == 0)
            sn = jnp.where(vis, sn, NEG)
            mx = jnp.maximum(jnp.max(sc, axis=0, keepdims=True), jnp.max(sn, axis=0, keepdims=True))
            pc = jnp.exp(sc - mx)
            pn = jnp.exp(sn - mx)
            den = jnp.sum(pc, axis=0, keepdims=True) + jnp.sum(pn, axis=0, keepdims=True)
            o = (jnp.sum(pc * vc, axis=0, keepdims=True)
                 + jnp.sum(pn * vn, axis=0, keepdims=True)) / den
            outs.append(o[0])
            lses.append((mx + jnp.log(den))[0])
        lmx = jnp.maximum(jnp.maximum(lses[0], lses[1]), lses[2])
        es = [jnp.exp(l - lmx) for l in lses]
        tot = es[0] + es[1] + es[2]
        o_ref[t] = outs[0] * (es[0] / tot) + outs[1] * (es[1] / tot) + outs[2] * (es[2] / tot)


def _attn_sample(q, k_new, v_new, caches, nbatch, t_new):
    shp = (nbatch, t_new, N_GROUPS, N_HEADS, HEAD_DIM)
    cviews, cspecs = [], []
    for (win, dil), c in zip(GROUPS, caches):
        n_c = win // dil
        n_res = min(dil, t_new)
        cviews.append(c.reshape(nbatch, n_c, dil, 2, N_HEADS, HEAD_DIM))
        cspecs.append(pl.BlockSpec((None, n_c, n_res, 2, N_HEADS, HEAD_DIM),
                                   lambda b: (b, 0, 0, 0, 0, 0)))
    new_spec = pl.BlockSpec((None, t_new, N_GROUPS, N_HEADS, HEAD_DIM), lambda b: (b, 0, 0, 0, 0))
    kern = functools.partial(_attn_sample_kernel, t_new=t_new)
    o = pl.pallas_call(
        kern,
        out_shape=jax.ShapeDtypeStruct((nbatch, t_new, N_HEADS, HEAD_DIM), F32),
        grid=(nbatch,),
        in_specs=[new_spec, new_spec, new_spec, *cspecs],
        out_specs=pl.BlockSpec((None, t_new, N_HEADS, HEAD_DIM), lambda b: (b, 0, 0, 0)),
        compiler_params=_params(1),
        name="attn_sample",
    )(q.reshape(shp), k_new.reshape(shp), v_new.reshape(shp), *cviews)
    return o.reshape(nbatch * t_new, N_HEADS * HEAD_DIM)


def _kv_tail_kernel(k_ref, v_ref, o_ref, *, rows):
    for kv, ref in enumerate((k_ref, v_ref)):
        for h in range(N_HEADS):
            o_ref[pl.ds(kv * N_HEADS + h, rows, stride=2 * N_HEADS), :] = (
                ref[:, h * HEAD_DIM:(h + 1) * HEAD_DIM])


def _kv_tail(k, v, g, batch, seq, win, tr=128):
    gw = N_HEADS * HEAD_DIM
    nblk = win // tr
    first = (seq - win) // tr
    per_seq = seq // tr
    in_spec = pl.BlockSpec((tr, gw), lambda b, i: (b * per_seq + first + i, g))
    flat = pl.pallas_call(
        functools.partial(_kv_tail_kernel, rows=tr),
        out_shape=jax.ShapeDtypeStruct((batch * win * 2 * N_HEADS, HEAD_DIM), F32),
        grid=(batch, nblk),
        in_specs=[in_spec, in_spec],
        out_specs=pl.BlockSpec((tr * 2 * N_HEADS, HEAD_DIM), lambda b, i: (b * nblk + i, 0)),
        compiler_params=_params(2),
        name="kv_tail",
    )(k, v)
    return flat.reshape(batch, win, 2, N_HEADS, HEAD_DIM)


def _cache_shift_kernel(a_ref, nxt_ref, new_ref, o_ref, *, rows, t_new):
    o_ref[0:rows - t_new] = a_ref[t_new:rows]
    last = pl.program_id(1) == pl.num_programs(1) - 1
    o_ref[rows - t_new:rows] = jnp.where(last, new_ref[...], nxt_ref[...])


def _cache_shift(cache, new, tr=256):
    nbatch, win = cache.shape[:2]
    t_new = new.shape[1]
    tr = min(tr, win)
    nblk = win // tr
    tail = cache.shape[2:]
    zeros = (0,) * len(tail)
    return pl.pallas_call(
        functools.partial(_cache_shift_kernel, rows=tr, t_new=t_new),
        out_shape=jax.ShapeDtypeStruct(cache.shape, cache.dtype),
        grid=(nbatch, nblk),
        in_specs=[
            pl.BlockSpec((None, tr) + tail, lambda b, i: (b, i) + zeros),
            pl.BlockSpec((None, t_new) + tail,
                         lambda b, i: (b, jnp.minimum(i + 1, nblk - 1) * (tr // t_new)) + zeros),
            pl.BlockSpec((None, t_new) + tail, lambda b, i: (b, 0) + zeros),
        ],
        out_specs=pl.BlockSpec((None, tr) + tail, lambda b, i: (b, i) + zeros),
        compiler_params=_params(2),
        name="cache_shift",
    )(cache, cache, new)


def _trunk(x, pos, batch, seq, conv_state, kv_caches, wts):
    m = x.shape[0]
    tm = min(512, m)
    tm_wide = min(1024, m)
    tabs = _rope_tables(pos)
    n_a = wts["conv_w_in"].shape[0]
    depth = wts["ffn_w_in"].shape[0]
    row = lambda v: v.reshape(1, -1)
    zero_bias = jnp.zeros((1, D_MODEL), F32)
    new_conv = []
    k = v = None
    for l in range(depth):
        if l < n_a:
            w_taps = jnp.pad(wts["conv_w_dw"][l],
                             ((CONV_HALO - CONV_STATE, CONV_TAPS - CONV_HALO - 1), (0, 0)))
            cargs = (w_taps, row(wts["conv_b_dw"][l]), row(wts["conv_ln_g"][l]), row(wts["conv_ln_b"][l]))
            g = _glu(x, row(wts["norm_mix"][l]), wts["conv_w_in"], row(wts["conv_b_in"][l]), l, tm_wide)
            g3 = g.reshape(batch, seq, D_MODEL)
            if conv_state is None:
                y = _conv_prompt(g, *cargs, seq=seq)
                new_conv.append(g3[:, seq - CONV_STATE:])
            else:
                buf = jnp.concatenate([conv_state[l], g3], axis=1)
                new_conv.append(buf[:, -CONV_STATE:])
                lead = CONV_HALO - CONV_STATE
                buf = jnp.pad(buf, ((0, 0), (lead, CONV_TAPS - lead - CONV_STATE - seq), (0, 0)))
                y = _conv_sample(buf, *cargs, t_new=seq).reshape(m, D_MODEL)
            x = _out_proj(y, wts["conv_w_out"], row(wts["conv_b_out"][l]), x, l, tm)
        else:
            j = l - n_a
            if j == 0:
                gkv = row(wts["norm_kv"])
                k = _proj_rope(x, gkv, wts["w_kv"], 0, 0, QW, tabs, tm_wide)
                v = _proj(x, gkv, wts["w_kv"], 0, QW, QW, tm_wide)
            q = _proj_rope(x, row(wts["norm_mix"][l]), wts["attn_w_q"], j, 0, QW, tabs, tm_wide)
            if kv_caches is None:
                o = _attn_prompt(q, k, v, batch, seq)
            else:
                o = _attn_sample(q, k, v, kv_caches, batch, seq)
            x = _out_proj(o, wts["attn_w_o"], zero_bias, x, j, tm)
        x = _ffn(x, row(wts["norm_ffn"][l]), wts["ffn_w_in"], wts["ffn_w_out"],
                 row(wts["norm_final"]), l, tm_wide, final_norm=(l == depth - 1))
    return x, jnp.stack(new_conv, axis=0), k.reshape(batch, seq, QW), v.reshape(batch, seq, QW)


def _kv_new_rows(k3, v3, g):
    gw = N_HEADS * HEAD_DIM
    shp = k3.shape[:2] + (N_HEADS, HEAD_DIM)
    return jnp.stack([a[:, :, g * gw:(g + 1) * gw].reshape(shp) for a in (k3, v3)], axis=2)


def kernel(x_prompt, x_sample, state_conv, cache_kv_g0, cache_kv_g1, cache_kv_g2, norm_mix, norm_ffn, conv_w_in, conv_b_in, conv_w_dw, conv_b_dw, conv_ln_g, conv_ln_b, conv_w_out, conv_b_out, norm_kv, w_kv, attn_w_q, attn_w_o, ffn_w_in, ffn_w_out, norm_final):
    wts = dict(norm_mix=norm_mix, norm_ffn=norm_ffn, conv_b_in=conv_b_in, conv_w_dw=conv_w_dw,
               conv_b_dw=conv_b_dw, conv_ln_g=conv_ln_g, conv_ln_b=conv_ln_b, conv_b_out=conv_b_out,
               norm_kv=norm_kv, norm_final=norm_final)
    for name, w in (("conv_w_in", conv_w_in), ("conv_w_out", conv_w_out), ("w_kv", w_kv[None]),
                    ("attn_w_q", attn_w_q), ("attn_w_o", attn_w_o), ("ffn_w_in", ffn_w_in),
                    ("ffn_w_out", ffn_w_out)):
        wts[name] = w.astype(BF16)

    bp, sp, d = x_prompt.shape
    pos_p = jnp.arange(sp, dtype=jnp.int32)
    y_p, conv_p, k_p, v_p = _trunk(x_prompt.reshape(bp * sp, d), pos_p, bp, sp, None, None, wts)
    kv_out_p = [_kv_tail(k_p.reshape(bp * sp, QW), v_p.reshape(bp * sp, QW), g, bp, sp, min(w, sp))
                for g, (w, _) in enumerate(GROUPS)]

    bs, ss, _ = x_sample.shape
    caches = (cache_kv_g0, cache_kv_g1, cache_kv_g2)
    pos_s = jnp.tile(PAST_LEN + jnp.arange(ss, dtype=jnp.int32), bs)
    y_s, conv_s, k_s, v_s = _trunk(x_sample.reshape(bs * ss, d), pos_s, bs, ss, state_conv, caches, wts)
    kv_out_s = [_cache_shift(c, _kv_new_rows(k_s, v_s, g)) for g, c in enumerate(caches)]

    return (y_p.reshape(bp, sp, d), y_s.reshape(bs, ss, d), conv_p, *kv_out_p, conv_s, *kv_out_s)
```

```python
import functools

import jax
import jax.numpy as jnp
from jax import lax
from jax.experimental import pallas as pl
from jax.experimental.pallas import tpu as pltpu

F32 = jnp.float32
BF16 = jnp.bfloat16

D_MODEL = 2048
N_HEADS = 16
HEAD_DIM = 128
GROUPS = ((128, 1), (512, 4), (2048, 16))
N_GROUPS = len(GROUPS)
PAST_LEN = 16384
QBLK = 128
ATT_SPAN = QBLK * max(d for _, d in GROUPS)
ROT_DIM = HEAD_DIM // 4
ROPE_THETA = 500000.0
CONV_WIDTH = 31
CONV_STATE = CONV_WIDTH - 1
CONV_HALO = 32
CONV_TAPS = 40
RMS_EPS = 1e-6
LN_EPS = 1e-5
NEG = -1e30
SM_SCALE = HEAD_DIM ** -0.5
QW = N_GROUPS * N_HEADS * HEAD_DIM

V7X_VMEM_LIMIT = 56 * 1024 * 1024


def _params(n_axes):
    return pltpu.CompilerParams(dimension_semantics=("arbitrary",) * n_axes,
                                vmem_limit_bytes=V7X_VMEM_LIMIT)


def _rms(x, g):
    return x * lax.rsqrt(jnp.mean(x * x, axis=-1, keepdims=True) + RMS_EPS) * g


def _glu_kernel(x_ref, gn_ref, wa_ref, wg_ref, ba_ref, bg_ref, o_ref, xn_ref):
    @pl.when(pl.program_id(1) == 0)
    def _():
        xn_ref[...] = _rms(x_ref[...], gn_ref[...]).astype(BF16)

    xn = xn_ref[...]
    a = jnp.dot(xn, wa_ref[...], preferred_element_type=F32) + ba_ref[...]
    gate = jnp.dot(xn, wg_ref[...], preferred_element_type=F32) + bg_ref[...]
    o_ref[...] = a * jax.nn.sigmoid(gate)


def _glu(x, gn, w_in, b_in, layer, tm, tn=1024):
    m, d = x.shape
    nj = d // tn
    return pl.pallas_call(
        _glu_kernel,
        out_shape=jax.ShapeDtypeStruct((m, d), F32),
        grid=(m // tm, nj),
        in_specs=[
            pl.BlockSpec((tm, d), lambda i, j: (i, 0)),
            pl.BlockSpec((1, d), lambda i, j: (0, 0)),
            pl.BlockSpec((None, d, tn), lambda i, j: (layer, 0, j)),
            pl.BlockSpec((None, d, tn), lambda i, j: (layer, 0, j + nj)),
            pl.BlockSpec((1, tn), lambda i, j: (0, j)),
            pl.BlockSpec((1, tn), lambda i, j: (0, j + nj)),
        ],
        out_specs=pl.BlockSpec((tm, tn), lambda i, j: (i, j)),
        scratch_shapes=[pltpu.VMEM((tm, d), BF16)],
        compiler_params=_params(2),
        name="glu_proj",
    )(x, gn, w_in, w_in, b_in, b_in)


def _conv_taps(buf_ref, w_ref, r0, c0, rows):
    acc = None
    for s in range(8):
        win = rows if s == 0 else rows + 8
        u = None
        for a in range(CONV_TAPS // 8):
            k = 8 * a + s
            if k < CONV_HALO - CONV_STATE or k > CONV_HALO:
                continue
            term = buf_ref[pl.ds(r0 + 8 * a, win), pl.ds(c0, 128)] * w_ref[k:k + 1, pl.ds(c0, 128)]
            u = term if u is None else u + term
        part = u if s == 0 else u[s:s + rows]
        acc = part if acc is None else acc + part
    return acc


def _ln_silu(y, g, b):
    mu = jnp.mean(y, axis=-1, keepdims=True)
    dlt = y - mu
    var = jnp.mean(dlt * dlt, axis=-1, keepdims=True)
    z = dlt * lax.rsqrt(var + LN_EPS) * g + b
    return z * jax.nn.sigmoid(z)


def _conv_prompt_kernel(cur_ref, halo_ref, w_ref, bdw_ref, lg_ref, lb_ref, o_ref,
                        buf_ref, y_ref, *, tiles_per_seq, tm, rc):
    i = pl.program_id(0)
    buf_ref[0:CONV_HALO, :] = jnp.where(i % tiles_per_seq == 0, 0.0, halo_ref[...])
    buf_ref[CONV_HALO:, :] = cur_ref[...]

    def col_body(c, carry):
        c0 = pl.multiple_of(c * 128, 128)

        def row_body(r, carry2):
            r0 = pl.multiple_of(r * rc, rc)
            y_ref[pl.ds(r0, rc), pl.ds(c0, 128)] = _conv_taps(buf_ref, w_ref, r0, c0, rc)
            return carry2

        return lax.fori_loop(0, tm // rc, row_body, carry)

    lax.fori_loop(0, D_MODEL // 128, col_body, 0)
    y = y_ref[...] + bdw_ref[...]
    o_ref[...] = _ln_silu(y, lg_ref[...], lb_ref[...]).astype(o_ref.dtype)


def _conv_prompt(g, w_taps, b_dw, ln_g, ln_b, seq, tm=512, rc=64):
    m, d = g.shape
    hb = tm // CONV_HALO
    kern = functools.partial(_conv_prompt_kernel, tiles_per_seq=seq // tm, tm=tm, rc=rc)
    return pl.pallas_call(
        kern,
        out_shape=jax.ShapeDtypeStruct((m, d), BF16),
        grid=(m // tm,),
        in_specs=[
            pl.BlockSpec((tm, d), lambda i: (i, 0)),
            pl.BlockSpec((CONV_HALO, d), lambda i: (jnp.maximum(i * hb - 1, 0), 0)),
            pl.BlockSpec((CONV_TAPS, d), lambda i: (0, 0)),
            pl.BlockSpec((1, d), lambda i: (0, 0)),
            pl.BlockSpec((1, d), lambda i: (0, 0)),
            pl.BlockSpec((1, d), lambda i: (0, 0)),
        ],
        out_specs=pl.BlockSpec((tm, d), lambda i: (i, 0)),
        scratch_shapes=[pltpu.VMEM((tm + CONV_HALO, d), F32), pltpu.VMEM((tm, d), F32)],
        compiler_params=_params(1),
        name="conv_prompt",
    )(g, g, w_taps, b_dw, ln_g, ln_b)


def _conv_sample_kernel(buf_ref, w_ref, bdw_ref, lg_ref, lb_ref, o_ref, *, t_new):
    acc = None
    for k in range(CONV_HALO - CONV_STATE, CONV_HALO + 1):
        term = buf_ref[k:k + t_new, :] * w_ref[k:k + 1, :]
        acc = term if acc is None else acc + term
    y = acc + bdw_ref[...]
    o_ref[...] = _ln_silu(y, lg_ref[...], lb_ref[...])


def _conv_sample(buf, w_taps, b_dw, ln_g, ln_b, t_new):
    nb, rows, d = buf.shape
    kern = functools.partial(_conv_sample_kernel, t_new=t_new)
    return pl.pallas_call(
        kern,
        out_shape=jax.ShapeDtypeStruct((nb, t_new, d), F32),
        grid=(nb,),
        in_specs=[
            pl.BlockSpec((None, rows, d), lambda b: (b, 0, 0)),
            pl.BlockSpec((CONV_TAPS, d), lambda b: (0, 0)),
            pl.BlockSpec((1, d), lambda b: (0, 0)),
            pl.BlockSpec((1, d), lambda b: (0, 0)),
            pl.BlockSpec((1, d), lambda b: (0, 0)),
        ],
        out_specs=pl.BlockSpec((None, t_new, d), lambda b: (b, 0, 0)),
        compiler_params=_params(1),
        name="conv_sample",
    )(buf, w_taps, b_dw, ln_g, ln_b)


def _out_proj_kernel(y_ref, w_ref, b_ref, x_ref, o_ref):
    acc = jnp.dot(y_ref[...].astype(BF16), w_ref[...], preferred_element_type=F32)
    o_ref[...] = x_ref[...] + acc + b_ref[...]


def _out_proj(y, w, b, x, layer, tm):
    m, d = x.shape
    k = y.shape[1]
    return pl.pallas_call(
        _out_proj_kernel,
        out_shape=jax.ShapeDtypeStruct((m, d), F32),
        grid=(m // tm,),
        in_specs=[
            pl.BlockSpec((tm, k), lambda i: (i, 0)),
            pl.BlockSpec((None, k, d), lambda i: (layer, 0, 0)),
            pl.BlockSpec((1, d), lambda i: (0, 0)),
            pl.BlockSpec((tm, d), lambda i: (i, 0)),
        ],
        out_specs=pl.BlockSpec((tm, d), lambda i: (i, 0)),
        compiler_params=_params(1),
        name="out_proj",
    )(y, w, b, x)


def _ffn_kernel(x_ref, gn_ref, wg_ref, wu_ref, wo_ref, gf_ref, *refs, final_norm, cast_next):
    if cast_next:
        src_in_ref, src_out_ref, o_ref, nxt_in_ref, nxt_out_ref, xn_ref = refs
        nxt_in_ref[...] = src_in_ref[...].astype(BF16)
        nxt_out_ref[...] = src_out_ref[...].astype(BF16)
    else:
        o_ref, xn_ref = refs
    f = pl.program_id(1)

    @pl.when(f == 0)
    def _():
        x = x_ref[...]
        xn_ref[...] = _rms(x, gn_ref[...]).astype(BF16)
        o_ref[...] = x

    xn = xn_ref[...]
    gt = jnp.dot(xn, wg_ref[...], preferred_element_type=F32)
    up = jnp.dot(xn, wu_ref[...], preferred_element_type=F32)
    h = (gt * jax.nn.sigmoid(gt) * up).astype(BF16)
    o_ref[...] += jnp.dot(h, wo_ref[...], preferred_element_type=F32)

    if final_norm:
        @pl.when(f == pl.num_programs(1) - 1)
        def _():
            o_ref[...] = _rms(o_ref[...], gf_ref[...])


def _ffn(x, gn, w_in, w_out, gf, tm, final_norm, nxt=None, tf=512):
    m, d = x.shape
    dff = w_out.shape[1]
    nf = dff // tf
    steps = (m // tm) * nf
    kern = functools.partial(_ffn_kernel, final_norm=final_norm, cast_next=nxt is not None)
    in_specs = [
        pl.BlockSpec((tm, d), lambda i, f: (i, 0)),
        pl.BlockSpec((1, d), lambda i, f: (0, 0)),
        pl.BlockSpec((None, d, tf), lambda i, f: (0, 0, f)),
        pl.BlockSpec((None, d, tf), lambda i, f: (0, 0, f + nf)),
        pl.BlockSpec((None, tf, d), lambda i, f: (0, f, 0)),
        pl.BlockSpec((1, d), lambda i, f: (0, 0)),
    ]
    out_shape = [jax.ShapeDtypeStruct((m, d), F32)]
    out_specs = [pl.BlockSpec((tm, d), lambda i, f: (i, 0))]
    args = [x, gn, w_in, w_in, w_out, gf]
    if nxt is not None:
        src_in, src_out, lyr = nxt
        in_specs[0] = pl.BlockSpec((tm, d), lambda i, f: (i, 0), pipeline_mode=pl.Buffered(1))
        cw, rh = 2 * dff // steps, dff // steps
        assert cw * steps == 2 * dff and cw % 128 == 0 and rh * steps == dff and rh % 16 == 0
        in_specs += [pl.BlockSpec((None, d, cw), lambda i, f: (lyr, 0, i * nf + f)),
                     pl.BlockSpec((None, rh, d), lambda i, f: (lyr, i * nf + f, 0))]
        out_shape += [jax.ShapeDtypeStruct((1, d, 2 * dff), BF16),
                      jax.ShapeDtypeStruct((1, dff, d), BF16)]
        out_specs += [pl.BlockSpec((None, d, cw), lambda i, f: (0, 0, i * nf + f)),
                      pl.BlockSpec((None, rh, d), lambda i, f: (0, i * nf + f, 0))]
        args += [src_in, src_out]
    outs = pl.pallas_call(
        kern,
        out_shape=out_shape,
        grid=(m // tm, nf),
        in_specs=in_specs,
        out_specs=out_specs,
        scratch_shapes=[pltpu.VMEM((tm, d), BF16)],
        compiler_params=_params(2),
        name="ffn",
    )(*args)
    return outs[0], tuple(outs[1:])


def _rope(acc, cos_t, sin_a, sin_b):
    outs = []
    for hh in range(acc.shape[1] // HEAD_DIM):
        xs = acc[:, hh * HEAD_DIM:(hh + 1) * HEAD_DIM]
        outs.append(xs * cos_t
                    + pltpu.roll(xs, HEAD_DIM - ROT_DIM // 2, axis=1) * sin_a
                    + pltpu.roll(xs, ROT_DIM // 2, axis=1) * sin_b)
    return jnp.concatenate(outs, axis=1)


def _proj_kernel(x_ref, gn_ref, w_ref, o_ref, xn_ref):
    @pl.when(pl.program_id(1) == 0)
    def _():
        xn_ref[...] = _rms(x_ref[...], gn_ref[...]).astype(BF16)

    o_ref[...] = jnp.dot(xn_ref[...], w_ref[...], preferred_element_type=F32)


def _proj(x, gn, w, layer, col0, n, tm, tn=1024):
    m, d = x.shape
    cb0 = col0 // tn
    return pl.pallas_call(
        _proj_kernel,
        out_shape=jax.ShapeDtypeStruct((m, n), F32),
        grid=(m // tm, n // tn),
        in_specs=[
            pl.BlockSpec((tm, d), lambda i, j: (i, 0)),
            pl.BlockSpec((1, d), lambda i, j: (0, 0)),
            pl.BlockSpec((None, d, tn), lambda i, j: (layer, 0, cb0 + j)),
        ],
        out_specs=pl.BlockSpec((tm, tn), lambda i, j: (i, j)),
        scratch_shapes=[pltpu.VMEM((tm, d), BF16)],
        compiler_params=_params(2),
        name="proj",
    )(x, gn, w)


def _proj_rope_kernel(x_ref, gn_ref, w_ref, cos_ref, sa_ref, sb_ref, o_ref, xn_ref, raw_ref, *, nj):
    t = pl.program_id(0)

    @pl.when(t % nj == 0)
    def _():
        xn_ref[...] = _rms(x_ref[...], gn_ref[...]).astype(BF16)

    @pl.when(t == 0)
    def _():
        raw_ref[...] = jnp.zeros_like(raw_ref)

    o_ref[...] = _rope(raw_ref[...], cos_ref[...], sa_ref[...], sb_ref[...])
    raw_ref[...] = jnp.dot(xn_ref[...], w_ref[...], preferred_element_type=F32)


def _proj_rope(x, gn, w, layer, col0, n, tabs, tm, tn=1024):
    m, d = x.shape
    cos_t, sin_a, sin_b = tabs
    tab_blocks = cos_t.shape[0] // tm
    cb0 = col0 // tn
    nj = n // tn
    tiles = (m // tm) * nj
    kern = functools.partial(_proj_rope_kernel, nj=nj)
    done = lambda t: jnp.maximum(t - 1, 0)
    tab_spec = pl.BlockSpec((tm, HEAD_DIM), lambda t: ((done(t) // nj) % tab_blocks, 0))
    return pl.pallas_call(
        kern,
        out_shape=jax.ShapeDtypeStruct((m, n), F32),
        grid=(tiles + 1,),
        in_specs=[
            pl.BlockSpec((tm, d), lambda t: (jnp.minimum(t, tiles - 1) // nj, 0)),
            pl.BlockSpec((1, d), lambda t: (0, 0)),
            pl.BlockSpec((None, d, tn), lambda t: (layer, 0, cb0 + t % nj)),
            tab_spec, tab_spec, tab_spec,
        ],
        out_specs=pl.BlockSpec((tm, tn), lambda t: (done(t) // nj, done(t) % nj)),
        scratch_shapes=[pltpu.VMEM((tm, d), BF16), pltpu.VMEM((tm, tn), F32)],
        compiler_params=_params(1),
        name="proj_rope",
    )(x, gn, w, cos_t, sin_a, sin_b)


def _rope_tables(pos):
    half = ROT_DIM // 2
    inv = ROPE_THETA ** (-jnp.arange(half, dtype=F32) / half)
    ang = pos.astype(F32)[:, None] * inv[None, :]
    cos, sin = jnp.cos(ang), jnp.sin(ang)
    n = pos.shape[0]
    ones = jnp.ones((n, HEAD_DIM - ROT_DIM), F32)
    zeros = jnp.zeros((n, HEAD_DIM - ROT_DIM), F32)
    zh = jnp.zeros((n, half), F32)
    cos_t = jnp.concatenate([cos, cos, ones], axis=1)
    sin_a = jnp.concatenate([-sin, zh, zeros], axis=1)
    sin_b = jnp.concatenate([zh, sin, zeros], axis=1)
    return cos_t, sin_a, sin_b


def _attn_prompt_kernel(q0, q1, q2, k0, k1, k2, v0, v1, v2, o_ref,
                        s_scr, m_scr, num_scr, den_scr, mx_scr, kcar_scr, vcar_scr):
    has_prev = pl.program_id(2) > 0

    @pl.when(jnp.logical_not(has_prev))
    def _():
        kcar_scr[...] = jnp.zeros_like(kcar_scr)
        vcar_scr[...] = jnp.zeros_like(vcar_scr)

    qi = lax.broadcasted_iota(jnp.int32, (QBLK, 2 * QBLK), 0)
    ci = lax.broadcasted_iota(jnp.int32, (QBLK, 2 * QBLK), 1)
    mask = jnp.logical_and(ci >= qi, ci <= qi + QBLK)
    mask0 = jnp.logical_and(mask, jnp.logical_or(ci >= QBLK, has_prev))
    ones = jnp.ones((2 * QBLK, HEAD_DIM), BF16)
    nt = (((1,), (1,)), ((), ()))
    refs = ((q0, k0, v0), (q1, k1, v1), (q2, k2, v2))
    car0 = 0

    for g, (_, dil) in enumerate(GROUPS):
        q_ref, kc_ref, vc_ref = refs[g]
        nblk = ATT_SPAN // (QBLK * dil)
        units = [(jb, r) for jb in range(nblk) for r in range(dil)]

        def rows(ref, jb, r, dil=dil):
            start = QBLK * jb * dil + r
            if dil == 1:
                return ref[pl.ds(start, QBLK), :]
            return ref[pl.ds(start, QBLK, stride=dil), :]

        def put(scr, jb, r, val, g=g, dil=dil):
            start = QBLK * jb * dil + r
            if dil == 1:
                scr[g, pl.ds(start, QBLK), :] = val
            else:
                scr[g, pl.ds(start, QBLK, stride=dil), :] = val

        for u, (jb, r) in enumerate(units):
            q = rows(q_ref, jb, r).astype(BF16)
            kc = rows(kc_ref, jb, r).astype(BF16)
            if jb > 0:
                kp, mk = rows(kc_ref, jb - 1, r).astype(BF16), mask
            else:
                kp, mk = kcar_scr[car0 + r], mask0
            if jb == nblk - 1:
                kcar_scr[car0 + r] = kc
            s = lax.dot_general(q, jnp.concatenate([kp, kc], axis=0), nt,
                                preferred_element_type=F32) * SM_SCALE
            s = jnp.where(mk, s, NEG)
            s_scr[u] = s
            mx = jnp.max(jnp.maximum(s[:, 0:QBLK], s[:, QBLK:]), axis=-1, keepdims=True)
            m_scr[u] = jnp.broadcast_to(mx, (QBLK, HEAD_DIM))

        for u, (jb, r) in enumerate(units):
            vc = rows(vc_ref, jb, r).astype(BF16)
            vp = rows(vc_ref, jb - 1, r).astype(BF16) if jb > 0 else vcar_scr[car0 + r]
            if jb == nblk - 1:
                vcar_scr[car0 + r] = vc
            mx = m_scr[u]
            p = jnp.concatenate([jnp.exp(s_scr[u, :, 0:QBLK] - mx), jnp.exp(s_scr[u, :, QBLK:] - mx)],
                                axis=1).astype(BF16)
            vals = jnp.concatenate([jnp.concatenate([vp, vc], axis=0), ones], axis=1)
            acc = jnp.dot(p, vals, preferred_element_type=F32)
            put(num_scr, jb, r, acc[:, 0:HEAD_DIM])
            put(den_scr, jb, r, acc[:, HEAD_DIM:])
            put(mx_scr, jb, r, mx)
        car0 += dil

    ch = 64
    for c in range(ATT_SPAN // ch):
        rs = pl.ds(c * ch, ch)
        m0, m1, m2 = mx_scr[0, rs, :], mx_scr[1, rs, :], mx_scr[2, rs, :]
        mm = jnp.maximum(jnp.maximum(m0, m1), m2)
        a0, a1, a2 = jnp.exp(m0 - mm), jnp.exp(m1 - mm), jnp.exp(m2 - mm)
        num = a0 * num_scr[0, rs, :] + a1 * num_scr[1, rs, :] + a2 * num_scr[2, rs, :]
        den = a0 * den_scr[0, rs, :] + a1 * den_scr[1, rs, :] + a2 * den_scr[2, rs, :]
        o_ref[rs, :] = (num / den).astype(o_ref.dtype)


def _attn_prompt(q, k, v, batch, seq):
    m = q.shape[0]
    assert seq % ATT_SPAN == 0
    nspan = seq // ATT_SPAN

    def cur(g):
        return pl.BlockSpec((ATT_SPAN, HEAD_DIM), lambda b, h, s: (b * nspan + s, g * N_HEADS + h))

    gs = range(N_GROUPS)
    n_carry = sum(d for _, d in GROUPS)
    acc = lambda: pltpu.VMEM((N_GROUPS, ATT_SPAN, HEAD_DIM), F32)
    car = lambda: pltpu.VMEM((n_carry, QBLK, HEAD_DIM), BF16)
    return pl.pallas_call(
        _attn_prompt_kernel,
        out_shape=jax.ShapeDtypeStruct((m, N_HEADS * HEAD_DIM), BF16),
        grid=(batch, N_HEADS, nspan),
        in_specs=[cur(g) for g in gs] * 3,
        out_specs=pl.BlockSpec((ATT_SPAN, HEAD_DIM), lambda b, h, s: (b * nspan + s, h)),
        scratch_shapes=[
            pltpu.VMEM((N_HEADS, QBLK, 2 * QBLK), F32),
            pltpu.VMEM((N_HEADS, QBLK, HEAD_DIM), F32),
            acc(), acc(), acc(), car(), car(),
        ],
        compiler_params=_params(3),
        name="attn_prompt",
    )(q, q, q, k, k, k, v, v, v)


def _attn_sample_kernel(q_ref, kn_ref, vn_ref, c0_ref, c1_ref, c2_ref, o_ref, *, t_new):
    caches = (c0_ref, c1_ref, c2_ref)
    for t in range(t_new):
        outs, lses = [], []
        for g, (win, dil) in enumerate(GROUPS):
            c_ref = caches[g]
            n_c = win // dil
            res = t % dil
            qg = q_ref[t, g]
            kc, vc = c_ref[:, res, 0], c_ref[:, res, 1]
            kn, vn = kn_ref[:, g], vn_ref[:, g]
            sc = jnp.sum(kc * qg[None], axis=-1, keepdims=True) * SM_SCALE
            sn = jnp.sum(kn * qg[None], axis=-1, keepdims=True) * SM_SCALE
            mi = lax.broadcasted_iota(jnp.int32, (n_c, N_HEADS, 1), 0)
            sc = jnp.where(mi * dil + res >= t, sc, NEG)
            ti = lax.broadcasted_iota(jnp.int32, (t_new, N_HEADS, 1), 0)
            vis = jnp.logical_and(ti <= t, ((t - ti) & (dil - 1)) == 0)
            sn = jnp.where(vis, sn, NEG)
            mx = jnp.maximum(jnp.max(sc, axis=0, keepdims=True), jnp.max(sn, axis=0, keepdims=True))
            pc = jnp.exp(sc - mx)
            pn = jnp.exp(sn - mx)
            den = jnp.sum(pc, axis=0, keepdims=True) + jnp.sum(pn, axis=0, keepdims=True)
            o = (jnp.sum(pc * vc, axis=0, keepdims=True)
                 + jnp.sum(pn * vn, axis=0, keepdims=True)) / den
            outs.append(o[0])
            lses.append((mx + jnp.log(den))[0])
        lmx = jnp.maximum(jnp.maximum(lses[0], lses[1]), lses[2])
        es = [jnp.exp(l - lmx) for l in lses]
        tot = es[0] + es[1] + es[2]
        o_ref[t] = outs[0] * (es[0] / tot) + outs[1] * (es[1] / tot) + outs[2] * (es[2] / tot)


def _attn_sample(q, k_new, v_new, caches, nbatch, t_new):
    shp = (nbatch, t_new, N_GROUPS, N_HEADS, HEAD_DIM)
    cviews, cspecs = [], []
    for (win, dil), c in zip(GROUPS, caches):
        n_c = win // dil
        n_res = min(dil, t_new)
        cviews.append(c.reshape(nbatch, n_c, dil, 2, N_HEADS, HEAD_DIM))
        cspecs.append(pl.BlockSpec((None, n_c, n_res, 2, N_HEADS, HEAD_DIM),
                                   lambda b: (b, 0, 0, 0, 0, 0)))
    new_spec = pl.BlockSpec((None, t_new, N_GROUPS, N_HEADS, HEAD_DIM), lambda b: (b, 0, 0, 0, 0))
    kern = functools.partial(_attn_sample_kernel, t_new=t_new)
    o = pl.pallas_call(
        kern,
        out_shape=jax.ShapeDtypeStruct((nbatch, t_new, N_HEADS, HEAD_DIM), F32),
        grid=(nbatch,),
        in_specs=[new_spec, new_spec, new_spec, *cspecs],
        out_specs=pl.BlockSpec((None, t_new, N_HEADS, HEAD_DIM), lambda b: (b, 0, 0, 0)),
        compiler_params=_params(1),
        name="attn_sample",
    )(q.reshape(shp), k_new.reshape(shp), v_new.reshape(shp), *cviews)
    return o.reshape(nbatch * t_new, N_HEADS * HEAD_DIM)


def _kv_tail_kernel(k_ref, v_ref, o_ref, *, rows):
    for kv, ref in enumerate((k_ref, v_ref)):
        for h in range(N_HEADS):
            o_ref[pl.ds(kv * N_HEADS + h, rows, stride=2 * N_HEADS), :] = (
                ref[:, h * HEAD_DIM:(h + 1) * HEAD_DIM])


def _kv_tail(k, v, g, batch, seq, win, tr=128):
    gw = N_HEADS * HEAD_DIM
    nblk = win // tr
    first = (seq - win) // tr
    per_seq = seq // tr
    in_spec = pl.BlockSpec((tr, gw), lambda b, i: (b * per_seq + first + i, g))
    flat = pl.pallas_call(
        functools.partial(_kv_tail_kernel, rows=tr),
        out_shape=jax.ShapeDtypeStruct((batch * win * 2 * N_HEADS, HEAD_DIM), F32),
        grid=(batch, nblk),
        in_specs=[in_spec, in_spec],
        out_specs=pl.BlockSpec((tr * 2 * N_HEADS, HEAD_DIM), lambda b, i: (b * nblk + i, 0)),
        compiler_params=_params(2),
        name="kv_tail",
    )(k, v)
    return flat.reshape(batch, win, 2, N_HEADS, HEAD_DIM)


def _cache_shift_kernel(a_ref, nxt_ref, new_ref, o_ref, *, rows, t_new):
    o_ref[0:rows - t_new] = a_ref[t_new:rows]
    last = pl.program_id(1) == pl.num_programs(1) - 1
    o_ref[rows - t_new:rows] = jnp.where(last, new_ref[...], nxt_ref[...])


def _cache_shift(cache, new, tr=256):
    nbatch, win = cache.shape[:2]
    t_new = new.shape[1]
    tr = min(tr, win)
    nblk = win // tr
    tail = cache.shape[2:]
    zeros = (0,) * len(tail)
    return pl.pallas_call(
        functools.partial(_cache_shift_kernel, rows=tr, t_new=t_new),
        out_shape=jax.ShapeDtypeStruct(cache.shape, cache.dtype),
        grid=(nbatch, nblk),
        in_specs=[
            pl.BlockSpec((None, tr) + tail, lambda b, i: (b, i) + zeros),
            pl.BlockSpec((None, t_new) + tail,
                         lambda b, i: (b, jnp.minimum(i + 1, nblk - 1) * (tr // t_new)) + zeros),
            pl.BlockSpec((None, t_new) + tail, lambda b, i: (b, 0) + zeros),
        ],
        out_specs=pl.BlockSpec((None, tr) + tail, lambda b, i: (b, i) + zeros),
        compiler_params=_params(2),
        name="cache_shift",
    )(cache, cache, new)


def _trunk(x, pos, batch, seq, conv_state, kv_caches, wts):
    m = x.shape[0]
    tm = min(512, m)
    tm_wide = min(1024, m)
    tabs = _rope_tables(pos)
    n_a = wts["conv_w_in"].shape[0]
    depth = wts["ffn_w_in"].shape[0]
    ffn_b16 = wts["ffn_b16"]
    row = lambda v: v.reshape(1, -1)
    zero_bias = jnp.zeros((1, D_MODEL), F32)
    new_conv = []
    k = v = None
    for l in range(depth):
        if l < n_a:
            w_taps = jnp.pad(wts["conv_w_dw"][l],
                             ((CONV_HALO - CONV_STATE, CONV_TAPS - CONV_HALO - 1), (0, 0)))
            cargs = (w_taps, row(wts["conv_b_dw"][l]), row(wts["conv_ln_g"][l]), row(wts["conv_ln_b"][l]))
            g = _glu(x, row(wts["norm_mix"][l]), wts["conv_w_in"], row(wts["conv_b_in"][l]), l, tm_wide)
            g3 = g.reshape(batch, seq, D_MODEL)
            if conv_state is None:
                y = _conv_prompt(g, *cargs, seq=seq)
                new_conv.append(g3[:, seq - CONV_STATE:])
            else:
                buf = jnp.concatenate([conv_state[l], g3], axis=1)
                new_conv.append(buf[:, -CONV_STATE:])
                lead = CONV_HALO - CONV_STATE
                buf = jnp.pad(buf, ((0, 0), (lead, CONV_TAPS - lead - CONV_STATE - seq), (0, 0)))
                y = _conv_sample(buf, *cargs, t_new=seq).reshape(m, D_MODEL)
            x = _out_proj(y, wts["conv_w_out"], row(wts["conv_b_out"][l]), x, l, tm)
        else:
            j = l - n_a
            if j == 0:
                gkv = row(wts["norm_kv"])
                k = _proj_rope(x, gkv, wts["w_kv"], 0, 0, QW, tabs, tm_wide)
                v = _proj(x, gkv, wts["w_kv"], 0, QW, QW, tm_wide)
            q = _proj_rope(x, row(wts["norm_mix"][l]), wts["attn_w_q"], j, 0, QW, tabs, tm_wide)
            if kv_caches is None:
                o = _attn_prompt(q, k, v, batch, seq)
            else:
                o = _attn_sample(q, k, v, kv_caches, batch, seq)
            x = _out_proj(o, wts["attn_w_o"], zero_bias, x, j, tm)
        w_in_b, w_out_b = ffn_b16[l]
        need_next = l + 1 < depth and l + 1 not in ffn_b16
        nxt = (wts["ffn_w_in"], wts["ffn_w_out"], l + 1) if need_next else None
        x, cast = _ffn(x, row(wts["norm_ffn"][l]), w_in_b, w_out_b, row(wts["norm_final"]),
                       tm_wide, final_norm=(l == depth - 1), nxt=nxt)
        if nxt is not None:
            ffn_b16[l + 1] = cast
    return x, jnp.stack(new_conv, axis=0), k.reshape(batch, seq, QW), v.reshape(batch, seq, QW)


def _kv_new_rows(k3, v3, g):
    gw = N_HEADS * HEAD_DIM
    shp = k3.shape[:2] + (N_HEADS, HEAD_DIM)
    return jnp.stack([a[:, :, g * gw:(g + 1) * gw].reshape(shp) for a in (k3, v3)], axis=2)


def kernel(x_prompt, x_sample, state_conv, cache_kv_g0, cache_kv_g1, cache_kv_g2, norm_mix, norm_ffn, conv_w_in, conv_b_in, conv_w_dw, conv_b_dw, conv_ln_g, conv_ln_b, conv_w_out, conv_b_out, norm_kv, w_kv, attn_w_q, attn_w_o, ffn_w_in, ffn_w_out, norm_final):
    wts = dict(norm_mix=norm_mix, norm_ffn=norm_ffn, conv_b_in=conv_b_in, conv_w_dw=conv_w_dw,
               conv_b_dw=conv_b_dw, conv_ln_g=conv_ln_g, conv_ln_b=conv_ln_b, conv_b_out=conv_b_out,
               norm_kv=norm_kv, norm_final=norm_final)
    for name, w in (("conv_w_in", conv_w_in), ("conv_w_out", conv_w_out), ("w_kv", w_kv[None]),
                    ("attn_w_q", attn_w_q), ("attn_w_o", attn_w_o)):
        wts[name] = w.astype(BF16)
    wts["ffn_w_in"], wts["ffn_w_out"] = ffn_w_in, ffn_w_out
    wts["ffn_b16"] = {0: (ffn_w_in[0:1].astype(BF16), ffn_w_out[0:1].astype(BF16))}

    bp, sp, d = x_prompt.shape
    pos_p = jnp.arange(sp, dtype=jnp.int32)
    y_p, conv_p, k_p, v_p = _trunk(x_prompt.reshape(bp * sp, d), pos_p, bp, sp, None, None, wts)
    kv_out_p = [_kv_tail(k_p.reshape(bp * sp, QW), v_p.reshape(bp * sp, QW), g, bp, sp, min(w, sp))
                for g, (w, _) in enumerate(GROUPS)]

    bs, ss, _ = x_sample.shape
    caches = (cache_kv_g0, cache_kv_g1, cache_kv_g2)
    pos_s = jnp.tile(PAST_LEN + jnp.arange(ss, dtype=jnp.int32), bs)
    y_s, conv_s, k_s, v_s = _trunk(x_sample.reshape(bs * ss, d), pos_s, bs, ss, state_conv, caches, wts)
    kv_out_s = [_cache_shift(c, _kv_new_rows(k_s, v_s, g)) for g, c in enumerate(caches)]

    return (y_p.reshape(bp, sp, d), y_s.reshape(bs, ss, d), conv_p, *kv_out_p, conv_s, *kv_out_s)
```
